```python
import math
import jax, jax.numpy as jnp
from jax import lax
import numpy as np

D_MODEL = 1024
BATCH = 8
SEQ = 8192
DEPTH = 1
DEC_BATCH = 128
DEC_SEQ = 8
PAST_LEN = 8192
PAGE_SIZE = 128

CONV_CH = D_MODEL // 2
CONV_TAPS = 31
N_HEADS = 4
HEAD_DIM = 64
V_DIM = 2 * HEAD_DIM
QK_WIDTH = N_HEADS * 2 * HEAD_DIM
ATTN_WIDTH = N_HEADS * V_DIM
ROT_DIM = HEAD_DIM // 4
ROPE_THETA = 500000.0
D_FF = 2816
FFN_TAPS = 3
Q_BLOCK = 128
ALPHA = (2 * DEPTH) ** 0.25
BETA = (8 * DEPTH) ** -0.25
LN_EPS = 1e-5
LAMBDA_STD = 0.1
SPLITS = [CONV_CH, 2 * CONV_CH, 2 * CONV_CH + QK_WIDTH, 2 * CONV_CH + 2 * QK_WIDTH,
          2 * CONV_CH + 2 * QK_WIDTH + ATTN_WIDTH, 2 * CONV_CH + 2 * QK_WIDTH + ATTN_WIDTH + D_MODEL]
IN_WIDTH = 2 * CONV_CH + 2 * QK_WIDTH + ATTN_WIDTH + 2 * D_MODEL

kernel_name = "conformer_diffattn_gated_hybrid_step"


def layer_norm(x, g, b):
    xf = x.astype(jnp.float32)
    mu = jnp.mean(xf, -1, keepdims=True)
    var = jnp.mean(jnp.square(xf - mu), -1, keepdims=True)
    return ((xf - mu) * lax.rsqrt(var + LN_EPS) * g + b).astype(x.dtype)


def rms_norm(x, g):
    xf = x.astype(jnp.float32)
    return (xf * lax.rsqrt(jnp.mean(jnp.square(xf), -1, keepdims=True) + LN_EPS) * g).astype(x.dtype)


def causal_dwconv(x_ext, w, b):
    c = w.shape[1]
    y = lax.conv_general_dilated(x_ext, w[:, None, :].astype(x_ext.dtype), window_strides=(1,), padding='VALID',
                                 dimension_numbers=('NWC', 'WIO', 'NWC'), feature_group_count=c)
    return y + b


def partial_rotary(t, pos):
    half = ROT_DIM // 2
    inv = ROPE_THETA ** (-jnp.arange(half, dtype=jnp.float32) / half)
    ang = pos.astype(jnp.float32)[:, None] * inv[None, :]
    cos = jnp.cos(ang)[:, None, None, :]
    sin = jnp.sin(ang)[:, None, None, :]
    tf = t.astype(jnp.float32)
    r1 = tf[..., :half]
    r2 = tf[..., half:ROT_DIM]
    out = jnp.concatenate([r1 * cos - r2 * sin, r2 * cos + r1 * sin, tf[..., ROT_DIM:]], -1)
    return out.astype(t.dtype)


def diff_attention_core(q, k_parts, v_parts, q_pos, k_pos, lam):
    scale = HEAD_DIM ** -0.5
    s = jnp.concatenate([jnp.einsum('bqhmd,bkhmd->bhmqk', q, k, preferred_element_type=jnp.float32)
                         for k in k_parts], -1) * scale
    mask = k_pos[None, :] <= q_pos[:, None]
    s = jnp.where(mask, s, -jnp.inf)
    p = jax.nn.softmax(s, axis=-1)
    a = p[:, :, 0] - lam * p[:, :, 1]
    out = None
    off = 0
    for v in v_parts:
        n = v.shape[1]
        o = jnp.einsum('bhqk,bkhv->bqhv', a[..., off:off + n], v, preferred_element_type=jnp.float32)
        out = o if out is None else out + o
        off += n
    return out.astype(q.dtype)


def prompt_diff_attention(q, k, v, lam):
    b, s = q.shape[:2]
    nb = s // Q_BLOCK
    qb = q.reshape(b, nb, Q_BLOCK, N_HEADS, 2, HEAD_DIM).transpose(1, 0, 2, 3, 4, 5)
    starts = jnp.arange(nb, dtype=jnp.int32) * Q_BLOCK
    k_pos = jnp.arange(s, dtype=jnp.int32)

    def block(args):
        q_blk, start = args
        q_pos = start + jnp.arange(Q_BLOCK, dtype=jnp.int32)
        return diff_attention_core(q_blk, (k,), (v,), q_pos, k_pos, lam)

    ob = lax.map(block, (qb, starts))
    return ob.transpose(1, 0, 2, 3, 4).reshape(b, s, N_HEADS, V_DIM)


def run_layer(x, pos, conv_ctx, ffn_ctx, attend, lam_init, w_in, conv_dw_w, conv_dw_b, conv_ln_g, conv_ln_b,
              w_br_a, subln_g, w_br_b, w_o, ln1_g, ln1_b, w_up, ffn_dw_w, ffn_dw_b, w_down, ln2_g, ln2_b):
    n, s, _ = x.shape
    proj = x @ w_in
    glu_val, glu_gate, q, k, v, gate_a, gate_b = jnp.split(proj, SPLITS, axis=-1)
    u = glu_val * jax.nn.sigmoid(glu_gate)
    u_ext = jnp.concatenate([conv_ctx.astype(u.dtype), u], axis=1)
    c = causal_dwconv(u_ext, conv_dw_w, conv_dw_b)
    c = jax.nn.silu(layer_norm(c, conv_ln_g, conv_ln_b))
    branch_a = c @ w_br_a
    new_conv = u_ext[:, -(CONV_TAPS - 1):]
    q = partial_rotary(q.reshape(n, s, N_HEADS, 2, HEAD_DIM), pos)
    k = partial_rotary(k.reshape(n, s, N_HEADS, 2, HEAD_DIM), pos)
    v = v.reshape(n, s, N_HEADS, V_DIM)
    o = attend(q, k, v)
    o = rms_norm(o, subln_g) * (1.0 - lam_init)
    branch_b = o.reshape(n, s, ATTN_WIDTH) @ w_br_b
    mixed = jax.nn.sigmoid(gate_a) * branch_a + jax.nn.sigmoid(gate_b) * branch_b
    h = layer_norm(ALPHA * x + mixed @ w_o, ln1_g, ln1_b)
    up = h @ w_up
    uf, vf = jnp.split(up, [D_FF], axis=-1)
    uf_ext = jnp.concatenate([ffn_ctx.astype(uf.dtype), uf], axis=1)
    cf = causal_dwconv(uf_ext, ffn_dw_w, ffn_dw_b)
    f = (jax.nn.gelu(cf) * vf) @ w_down
    out = layer_norm(ALPHA * h + f, ln2_g, ln2_b)
    new_ffn = uf_ext[:, -(FFN_TAPS - 1):]
    return out, k, v, new_conv, new_ffn


def setup_inputs(seed: int = 0) -> dict:
    key = jax.random.key(seed)
    ks = jax.random.split(key, 32)
    f32 = jnp.float32
    n_pages = PAST_LEN // PAGE_SIZE
    used = DEC_BATCH * n_pages
    n_phys = used + max(1, used // 4)
    nrm = lambda k, shape, sc: jax.random.normal(k, shape, f32) * sc
    page_table = jax.random.permutation(ks[0], n_phys)[:used].reshape(DEC_BATCH, n_pages).astype(jnp.int32)
    return {
        "x_prompt": nrm(ks[1], (BATCH, SEQ, D_MODEL), 1.0),
        "x_sample": nrm(ks[2], (DEC_BATCH, DEC_SEQ, D_MODEL), 1.0),
        "cache_k": nrm(ks[3], (DEPTH, n_phys, PAGE_SIZE, N_HEADS, 2, HEAD_DIM), 1.0),
        "cache_v": nrm(ks[4], (DEPTH, n_phys, PAGE_SIZE, N_HEADS, V_DIM), 1.0),
        "state_conv": nrm(ks[5], (DEPTH, DEC_BATCH, CONV_TAPS - 1, CONV_CH), 0.5),
        "state_ffn": nrm(ks[6], (DEPTH, DEC_BATCH, FFN_TAPS - 1, D_FF), 1.0),
        "page_table": page_table,
        "w_in": nrm(ks[7], (DEPTH, D_MODEL, IN_WIDTH), D_MODEL ** -0.5),
        "lambda_q1": nrm(ks[8], (DEPTH, HEAD_DIM), LAMBDA_STD),
        "lambda_k1": nrm(ks[9], (DEPTH, HEAD_DIM), LAMBDA_STD),
        "lambda_q2": nrm(ks[10], (DEPTH, HEAD_DIM), LAMBDA_STD),
        "lambda_k2": nrm(ks[11], (DEPTH, HEAD_DIM), LAMBDA_STD),
        "conv_dw_w": nrm(ks[12], (DEPTH, CONV_TAPS, CONV_CH), CONV_TAPS ** -0.5),
        "conv_dw_b": nrm(ks[13], (DEPTH, CONV_CH), 0.02),
        "conv_ln_g": 1.0 + nrm(ks[14], (DEPTH, CONV_CH), 0.02),
        "conv_ln_b": nrm(ks[15], (DEPTH, CONV_CH), 0.02),
        "w_br_a": nrm(ks[16], (DEPTH, CONV_CH, D_MODEL), CONV_CH ** -0.5),
        "subln_g": 1.0 + nrm(ks[17], (DEPTH, V_DIM), 0.02),
        "w_br_b": nrm(ks[18], (DEPTH, ATTN_WIDTH, D_MODEL), ATTN_WIDTH ** -0.5),
        "w_o": nrm(ks[19], (DEPTH, D_MODEL, D_MODEL), BETA * D_MODEL ** -0.5),
        "ln1_g": 1.0 + nrm(ks[20], (DEPTH, D_MODEL), 0.02),
        "ln1_b": nrm(ks[21], (DEPTH, D_MODEL), 0.02),
        "w_up": nrm(ks[22], (DEPTH, D_MODEL, 2 * D_FF), D_MODEL ** -0.5),
        "ffn_dw_w": nrm(ks[23], (DEPTH, FFN_TAPS, D_FF), FFN_TAPS ** -0.5),
        "ffn_dw_b": nrm(ks[24], (DEPTH, D_FF), 0.02),
        "w_down": nrm(ks[25], (DEPTH, D_FF, D_MODEL), BETA * D_FF ** -0.5),
        "ln2_g": 1.0 + nrm(ks[26], (DEPTH, D_MODEL), 0.02),
        "ln2_b": nrm(ks[27], (DEPTH, D_MODEL), 0.02),
    }


def reference(x_prompt, x_sample, cache_k, cache_v, state_conv, state_ffn, page_table,
              w_in, lambda_q1, lambda_k1, lambda_q2, lambda_k2, conv_dw_w, conv_dw_b, conv_ln_g, conv_ln_b,
              w_br_a, subln_g, w_br_b, w_o, ln1_g, ln1_b, w_up, ffn_dw_w, ffn_dw_b, w_down, ln2_g, ln2_b):
    b, s, _ = x_prompt.shape
    db, t, _ = x_sample.shape
    n_pages = page_table.shape[1]
    past = n_pages * PAGE_SIZE
    pos_p = jnp.arange(s, dtype=jnp.int32)
    pos_s = past + jnp.arange(t, dtype=jnp.int32)
    k_pos_s = jnp.concatenate([jnp.arange(past, dtype=jnp.int32), pos_s])
    conv_zero = jnp.zeros((b, CONV_TAPS - 1, CONV_CH), x_prompt.dtype)
    ffn_zero = jnp.zeros((b, FFN_TAPS - 1, D_FF), x_prompt.dtype)
    yp, ys = x_prompt, x_sample
    kp_l, vp_l, cp_l, fp_l, ks_l, vs_l, cs_l, fs_l = [], [], [], [], [], [], [], []
    for l in range(DEPTH):
        lam_init = 0.8 - 0.6 * math.exp(-0.3 * l)
        lam = (jnp.exp(jnp.sum(lambda_q1[l].astype(jnp.float32) * lambda_k1[l].astype(jnp.float32)))
               - jnp.exp(jnp.sum(lambda_q2[l].astype(jnp.float32) * lambda_k2[l].astype(jnp.float32)))
               + lam_init)
        weights = (w_in[l], conv_dw_w[l], conv_dw_b[l], conv_ln_g[l], conv_ln_b[l], w_br_a[l], subln_g[l],
                   w_br_b[l], w_o[l], ln1_g[l], ln1_b[l], w_up[l], ffn_dw_w[l], ffn_dw_b[l], w_down[l],
                   ln2_g[l], ln2_b[l])
        attend_p = lambda q, k, v, lam=lam: prompt_diff_attention(q, k, v, lam)
        yp, kp, vp, cp, fp = run_layer(yp, pos_p, conv_zero, ffn_zero, attend_p, lam_init, *weights)
        k_past = cache_k[l][page_table].reshape(db, past, N_HEADS, 2, HEAD_DIM)
        v_past = cache_v[l][page_table].reshape(db, past, N_HEADS, V_DIM)
        attend_s = lambda q, k, v, lam=lam, k_past=k_past, v_past=v_past: diff_attention_core(
            q, (k_past, k), (v_past, v), pos_s, k_pos_s, lam)
        ys, ksm, vsm, csm, fsm = run_layer(ys, pos_s, state_conv[l], state_ffn[l], attend_s, lam_init, *weights)
        kp_l.append(kp); vp_l.append(vp); cp_l.append(cp); fp_l.append(fp)
        ks_l.append(ksm); vs_l.append(vsm); cs_l.append(csm); fs_l.append(fsm)
    k_prompt = jnp.stack(kp_l)
    v_prompt = jnp.stack(vp_l)
    conv_prompt = jnp.stack(cp_l)
    ffn_prompt = jnp.stack(fp_l)
    k_sample = jnp.stack(ks_l)
    v_sample = jnp.stack(vs_l)
    conv_sample = jnp.stack(cs_l)
    ffn_sample = jnp.stack(fs_l)
    return (yp, ys, k_prompt, v_prompt, conv_prompt, ffn_prompt, k_sample, v_sample, conv_sample, ffn_sample)
```

```python
import functools
import math

import jax
import jax.numpy as jnp
import numpy as np
from jax import lax
from jax.experimental import pallas as pl
from jax.experimental.pallas import tpu as pltpu

D_MODEL = 1024
CONV_CH = D_MODEL // 2
CONV_TAPS = 31
N_HEADS = 4
HEAD_DIM = 64
V_DIM = 2 * HEAD_DIM
QK_WIDTH = N_HEADS * 2 * HEAD_DIM
ATTN_WIDTH = N_HEADS * V_DIM
ROT_DIM = HEAD_DIM // 4
ROPE_THETA = 500000.0
D_FF = 2816
FFN_TAPS = 3
LN_EPS = 1e-5
PAGE_SIZE = 128
QKVU_WIDTH = 2 * CONV_CH + 2 * QK_WIDTH + ATTN_WIDTH
GATE_WIDTH = 2 * D_MODEL

LANES = 128
SUBLANES = 8
VMEM_LIMIT = 56 * 1024 * 1024
CONV_PAD = 32
FFN_PAD = 8
NEG_BIG = -1e30

BF16 = jnp.bfloat16
F32 = jnp.float32


def _const_spec(shape):
    nd = len(shape)
    return pl.BlockSpec(shape, lambda *_: (0,) * nd, pipeline_mode=pl.Buffered(1))


def _layer_norm(x, g, b):
    mu = jnp.mean(x, axis=-1, keepdims=True)
    xc = x - mu
    var = jnp.mean(xc * xc, axis=-1, keepdims=True)
    return xc * lax.rsqrt(var + LN_EPS) * g + b


def _dot(a, b):
    return jnp.dot(a, b, preferred_element_type=F32)


def _dot_nt(a, b):
    return lax.dot_general(a, b, (((1,), (1,)), ((), ())), preferred_element_type=F32)


def _in_proj_kernel(x_ref, w_ref, ctx_ref, cos_ref, sa_ref, sb_ref, dww_ref, dwb_ref, lng_ref, lnb_ref,
                    q_ref, kf_ref, vf_ref, kb_ref, vb_ref, c_ref, nconv_ref, uext_ref, *, nb, tl, n_len):
    s = pl.program_id(1)
    lo = CONV_PAD - (CONV_TAPS - 1)

    @pl.when(s == 0)
    def _():
        uext_ref[:, lo:CONV_PAD, :] = ctx_ref[...]

    proj = _dot(x_ref[...].astype(BF16), w_ref[...])
    u = proj[:, :CONV_CH] * jax.nn.sigmoid(proj[:, CONV_CH:2 * CONV_CH])
    uext_ref[:, CONV_PAD:CONV_PAD + tl, :] = u.reshape(nb, tl, CONV_CH)

    acc = jnp.broadcast_to(dwb_ref[...].reshape(1, 1, CONV_CH), (nb, tl, CONV_CH))
    for j in range(CONV_TAPS):
        acc = acc + dww_ref[j:j + 1, :].reshape(1, 1, CONV_CH) * uext_ref[:, lo + j:lo + j + tl, :]
    c = _layer_norm(acc.reshape(nb * tl, CONV_CH), lng_ref[...], lnb_ref[...])
    c_ref[...] = (c * jax.nn.sigmoid(c)).astype(BF16)

    nconv_ref[...] = uext_ref[:, lo + tl:CONV_PAD + tl, :]
    if n_len > 1:
        @pl.when(s + 1 < n_len)
        def _():
            uext_ref[:, lo:CONV_PAD, :] = uext_ref[:, lo + tl:CONV_PAD + tl, :]

    cos = cos_ref[...]
    sa = sa_ref[...]
    sb = sb_ref[...]
    half = ROT_DIM // 2
    q0 = 2 * CONV_CH
    k0 = q0 + QK_WIDTH
    v0 = k0 + QK_WIDTH
    scale = HEAD_DIM ** -0.5
    for h in range(N_HEADS):
        c0, c1 = h * LANES, (h + 1) * LANES
        t = proj[:, q0 + c0:q0 + c1]
        r = t * cos + pltpu.roll(t, LANES - half, axis=1) * sa + pltpu.roll(t, half, axis=1) * sb
        q_ref[:, c0:c1] = (r * scale).astype(BF16)
        t = proj[:, k0 + c0:k0 + c1]
        r = t * cos + pltpu.roll(t, LANES - half, axis=1) * sa + pltpu.roll(t, half, axis=1) * sb
        kf_ref[:, c0:c1] = r
        kb_ref[:, c0:c1] = r.astype(BF16)
    v = proj[:, v0:v0 + ATTN_WIDTH]
    vf_ref[...] = v
    vb_ref[...] = v.astype(BF16)


def _in_proj(x2, ctx, w_qkvu, cos, sa, sb, dww, dwb, lng, lnb, *, nb, tl):
    n = ctx.shape[0]
    length = x2.shape[0] // n
    assert n % nb == 0 and length % tl == 0
    n_seq, n_len = n // nb, length // tl
    assert n_len == 1 or (nb == 1 and tl >= CONV_TAPS - 1)
    m = nb * tl
    row_map = lambda i, s: (i * n_len + s, 0)
    tab_spec = pl.BlockSpec((m, LANES), lambda i, s: (s, 0))
    seq_spec = pl.BlockSpec((nb, CONV_TAPS - 1, CONV_CH), lambda i, s: (i, 0, 0))
    out_rows = lambda width: pl.BlockSpec((m, width), row_map)
    rows = n * length
    return pl.pallas_call(
        functools.partial(_in_proj_kernel, nb=nb, tl=tl, n_len=n_len),
        grid=(n_seq, n_len),
        in_specs=[
            pl.BlockSpec((m, D_MODEL), row_map),
            _const_spec((D_MODEL, QKVU_WIDTH)),
            seq_spec,
            tab_spec, tab_spec, tab_spec,
            _const_spec((CONV_TAPS, CONV_CH)),
            _const_spec((1, CONV_CH)), _const_spec((1, CONV_CH)), _const_spec((1, CONV_CH)),
        ],
        out_specs=[out_rows(QK_WIDTH), out_rows(QK_WIDTH), out_rows(ATTN_WIDTH), out_rows(QK_WIDTH),
                   out_rows(ATTN_WIDTH), out_rows(CONV_CH), seq_spec],
        out_shape=[
            jax.ShapeDtypeStruct((rows, QK_WIDTH), BF16),
            jax.ShapeDtypeStruct((rows, QK_WIDTH), F32),
            jax.ShapeDtypeStruct((rows, ATTN_WIDTH), F32),
            jax.ShapeDtypeStruct((rows, QK_WIDTH), BF16),
            jax.ShapeDtypeStruct((rows, ATTN_WIDTH), BF16),
            jax.ShapeDtypeStruct((rows, CONV_CH), BF16),
            jax.ShapeDtypeStruct((n, CONV_TAPS - 1, CONV_CH), F32),
        ],
        scratch_shapes=[pltpu.VMEM((nb, CONV_PAD + tl, CONV_CH), F32)],
        compiler_params=pltpu.CompilerParams(dimension_semantics=("arbitrary", "arbitrary"),
                                             vmem_limit_bytes=VMEM_LIMIT),
        name="in_proj",
    )(x2, w_qkvu, ctx, cos, sa, sb, dww, dwb, lng, lnb)


def _subln(o1, l1, o2, l2, lam, g, out_scale):
    o = o1 / l1 - lam * (o2 / l2)
    ms = jnp.mean(o * o, axis=-1, keepdims=True)
    return o * lax.rsqrt(ms + LN_EPS) * g * out_scale


def _online_softmax_step(s, m_prev, l_prev):
    m_next = jnp.maximum(m_prev, jnp.max(s, axis=1, keepdims=True))
    p = jnp.exp(s - m_next[:, :1])
    alpha = jnp.exp(m_prev - m_next)
    l_next = alpha * l_prev + jnp.sum(p, axis=1, keepdims=True)
    return p, alpha, m_next, l_next


def _prompt_attn_kernel(qi_ref, ki_ref, lam_ref, q_ref, k_ref, v_ref, g_ref, o_ref,
                        qz_ref, m_ref, l_ref, acc_ref, *, bq, bk, out_scale):
    p_idx = pl.program_id(1)
    qi = qi_ref[p_idx]
    ki = ki_ref[p_idx]
    last_ki = (qi * bq + bq - 1) // bk

    @pl.when(ki == 0)
    def _():
        lane = lax.broadcasted_iota(jnp.int32, (bq, LANES), 1)
        for h in range(N_HEADS):
            qh = q_ref[:, h * LANES:(h + 1) * LANES]
            qz_ref[2 * h] = jnp.where(lane < HEAD_DIM, qh, jnp.zeros_like(qh))
            qz_ref[2 * h + 1] = jnp.where(lane >= HEAD_DIM, qh, jnp.zeros_like(qh))
        m_ref[...] = jnp.full(m_ref.shape, NEG_BIG, F32)
        l_ref[...] = jnp.zeros(l_ref.shape, F32)
        acc_ref[...] = jnp.zeros(acc_ref.shape, F32)

    def body(masked):
        if masked:
            qpos = qi * bq + lax.broadcasted_iota(jnp.int32, (bq, bk), 0)
            kpos = ki * bk + lax.broadcasted_iota(jnp.int32, (bq, bk), 1)
            visible = kpos <= qpos
        for h in range(N_HEADS):
            kh = k_ref[:, h * LANES:(h + 1) * LANES]
            vh = v_ref[:, h * LANES:(h + 1) * LANES]
            for mp in range(2):
                r = 2 * h + mp
                s = _dot_nt(qz_ref[r], kh)
                if masked:
                    s = jnp.where(visible, s, NEG_BIG)
                p, alpha, m_next, l_next = _online_softmax_step(s, m_ref[r], l_ref[r])
                m_ref[r] = m_next
                l_ref[r] = l_next
                acc_ref[r] = alpha * acc_ref[r] + _dot(p.astype(BF16), vh)

    on_diag = ki * bk + bk - 1 > qi * bq

    @pl.when(on_diag)
    def _():
        body(True)

    @pl.when(jnp.logical_not(on_diag))
    def _():
        body(False)

    @pl.when(ki == last_ki)
    def _():
        lam = lam_ref[0]
        for h in range(N_HEADS):
            o = _subln(acc_ref[2 * h], l_ref[2 * h], acc_ref[2 * h + 1], l_ref[2 * h + 1], lam,
                       g_ref[...], out_scale)
            o_ref[:, h * LANES:(h + 1) * LANES] = o.astype(BF16)


def _prompt_attn(q, k, v, lam, subln_g, out_scale, *, bq, bk):
    b, s, _ = q.shape
    assert s % bq == 0 and s % bk == 0
    pairs = [(i, j) for i in range(s // bq) for j in range((i * bq + bq - 1) // bk + 1)]
    qi_tab = jnp.asarray(np.array([p[0] for p in pairs], np.int32))
    ki_tab = jnp.asarray(np.array([p[1] for p in pairs], np.int32))
    q_spec = pl.BlockSpec((None, bq, QK_WIDTH), lambda bi, p, qt, kt: (bi, qt[p], 0))
    kv_spec = pl.BlockSpec((None, bk, QK_WIDTH), lambda bi, p, qt, kt: (bi, kt[p], 0))
    grid_spec = pltpu.PrefetchScalarGridSpec(
        num_scalar_prefetch=2,
        grid=(b, len(pairs)),
        in_specs=[
            pl.BlockSpec(memory_space=pltpu.SMEM),
            q_spec, kv_spec, kv_spec,
            pl.BlockSpec((1, V_DIM), lambda bi, p, qt, kt: (0, 0)),
        ],
        out_specs=q_spec,
        scratch_shapes=[
            pltpu.VMEM((2 * N_HEADS, bq, LANES), BF16),
            pltpu.VMEM((2 * N_HEADS, bq, LANES), F32),
            pltpu.VMEM((2 * N_HEADS, bq, LANES), F32),
            pltpu.VMEM((2 * N_HEADS, bq, V_DIM), F32),
        ],
    )
    return pl.pallas_call(
        functools.partial(_prompt_attn_kernel, bq=bq, bk=bk, out_scale=out_scale),
        grid_spec=grid_spec,
        out_shape=jax.ShapeDtypeStruct((b, s, ATTN_WIDTH), BF16),
        compiler_params=pltpu.CompilerParams(dimension_semantics=("arbitrary", "arbitrary"),
                                             vmem_limit_bytes=VMEM_LIMIT),
        name="prompt_attn",
    )(qi_tab, ki_tab, lam, q, k, v, subln_g)


def _decode_attn_kernel(pt_ref, lam_ref, q_ref, kn_ref, vn_ref, g_ref, *refs, n_grp, grp, t_new, out_scale):
    k_refs = refs[:grp]
    v_refs = refs[grp:2 * grp]
    o_ref, qbd_ref, m_ref, l_ref, acc_ref = refs[2 * grp:]
    g = pl.program_id(1)
    n_rows = 2 * N_HEADS * t_new

    @pl.when(g == 0)
    def _():
        qf = q_ref[...].astype(F32)
        lane = lax.broadcasted_iota(jnp.int32, (t_new, QK_WIDTH), 1)
        parts = [jnp.where((lane >= r * HEAD_DIM) & (lane < (r + 1) * HEAD_DIM), qf, 0.0)
                 for r in range(2 * N_HEADS)]
        qbd_ref[...] = jnp.concatenate(parts, axis=0).astype(BF16)
        m_ref[...] = jnp.full(m_ref.shape, NEG_BIG, F32)
        l_ref[...] = jnp.zeros(l_ref.shape, F32)
        acc_ref[...] = jnp.zeros(acc_ref.shape, F32)

    def update(kb, vb, visible):
        s = _dot_nt(qbd_ref[...], kb)
        if visible is not None:
            s = jnp.where(visible, s, NEG_BIG)
        p, alpha, m_next, l_next = _online_softmax_step(s, m_ref[...], l_ref[...])
        m_ref[...] = m_next
        l_ref[...] = l_next
        acc_ref[...] = jnp.concatenate([alpha] * N_HEADS, axis=1) * acc_ref[...] + _dot(p.astype(BF16), vb)

    kb = jnp.concatenate([r[...].astype(BF16) for r in k_refs], axis=0)
    vb = jnp.concatenate([r[...].astype(BF16) for r in v_refs], axis=0)
    update(kb, vb, None)

    @pl.when(g == n_grp - 1)
    def _():
        pad = jnp.zeros((LANES - t_new, QK_WIDTH), F32)
        kn = jnp.concatenate([kn_ref[...], pad], axis=0).astype(BF16)
        vn = jnp.concatenate([vn_ref[...], pad], axis=0).astype(BF16)
        row_t = lax.broadcasted_iota(jnp.int32, (n_rows, LANES), 0) % t_new
        col = lax.broadcasted_iota(jnp.int32, (n_rows, LANES), 1)
        update(kn, vn, col <= row_t)
        lam = lam_ref[0]
        for h in range(N_HEADS):
            r1 = slice(2 * h * t_new, (2 * h + 1) * t_new)
            r2 = slice((2 * h + 1) * t_new, (2 * h + 2) * t_new)
            cs = slice(h * LANES, (h + 1) * LANES)
            o = _subln(acc_ref[r1, cs], l_ref[r1, :], acc_ref[r2, cs], l_ref[r2, :], lam, g_ref[...], out_scale)
            o_ref[:, cs] = o.astype(BF16)


def _decode_attn(q, k_new, v_new, cache_k, cache_v, page_table, lam, subln_g, out_scale, *, grp):
    n, t_new, _ = q.shape
    n_pages = page_table.shape[1]
    assert n_pages % grp == 0
    n_grp = n_pages // grp
    n_rows = 2 * N_HEADS * t_new
    tok_spec = pl.BlockSpec((None, t_new, QK_WIDTH), lambda i, g, pt: (i, 0, 0))

    def page_spec(j):
        return pl.BlockSpec((None, PAGE_SIZE, QK_WIDTH), lambda i, g, pt: (pt[i, g * grp + j], 0, 0))

    grid_spec = pltpu.PrefetchScalarGridSpec(
        num_scalar_prefetch=1,
        grid=(n, n_grp),
        in_specs=[pl.BlockSpec(memory_space=pltpu.SMEM), tok_spec, tok_spec, tok_spec,
                  pl.BlockSpec((1, V_DIM), lambda i, g, pt: (0, 0))]
                 + [page_spec(j) for j in range(grp)] + [page_spec(j) for j in range(grp)],
        out_specs=tok_spec,
        scratch_shapes=[
            pltpu.VMEM((n_rows, QK_WIDTH), BF16),
            pltpu.VMEM((n_rows, LANES), F32),
            pltpu.VMEM((n_rows, LANES), F32),
            pltpu.VMEM((n_rows, ATTN_WIDTH), F32),
        ],
    )
    return pl.pallas_call(
        functools.partial(_decode_attn_kernel, n_grp=n_grp, grp=grp, t_new=t_new, out_scale=out_scale),
        grid_spec=grid_spec,
        out_shape=jax.ShapeDtypeStruct((n, t_new, ATTN_WIDTH), BF16),
        compiler_params=pltpu.CompilerParams(dimension_semantics=("arbitrary", "arbitrary"),
                                             vmem_limit_bytes=VMEM_LIMIT),
        name="decode_attn",
    )(page_table, lam, q, k_new, v_new, subln_g, *([cache_k] * grp), *([cache_v] * grp))


def _gelu_tanh(x):
    return 0.5 * x * (1.0 + jnp.tanh(math.sqrt(2.0 / math.pi) * (x + 0.044715 * (x * x * x))))


def _merge_ffn_kernel(x_ref, c_ref, o_ref, ctx_ref, wg_ref, wa_ref, wb_ref, wo_ref, g1_ref, b1_ref,
                      wup_ref, fw_ref, fb_ref, wdn_ref, g2_ref, b2_ref,
                      y_ref, nffn_ref, uext_ref, *, nb, tl, n_len, n_chunk, alpha):
    s = pl.program_id(1)
    lo = FFN_PAD - (FFN_TAPS - 1)
    m = nb * tl

    @pl.when(s == 0)
    def _():
        uext_ref[:, lo:FFN_PAD, :] = ctx_ref[...]

    x = x_ref[...]
    gates = _dot(x.astype(BF16), wg_ref[...])
    mixed = (jax.nn.sigmoid(gates[:, :D_MODEL]) * _dot(c_ref[...], wa_ref[...])
             + jax.nn.sigmoid(gates[:, D_MODEL:]) * _dot(o_ref[...], wb_ref[...]))
    h = _layer_norm(alpha * x + _dot(mixed.astype(BF16), wo_ref[...]), g1_ref[...], b1_ref[...])
    hb = h.astype(BF16)

    cw = D_FF // n_chunk
    f = jnp.zeros((m, D_MODEL), F32)
    for ci in range(n_chunk):
        cs = slice(ci * cw, (ci + 1) * cw)
        uf = _dot(hb, wup_ref[:, ci * cw:(ci + 1) * cw])
        vf = _dot(hb, wup_ref[:, D_FF + ci * cw:D_FF + (ci + 1) * cw])
        uf3 = uf.reshape(nb, tl, cw)
        uext_ref[:, FFN_PAD:FFN_PAD + tl, cs] = uf3
        cf = fb_ref[:, cs].reshape(1, 1, cw) + fw_ref[FFN_TAPS - 1:FFN_TAPS, cs].reshape(1, 1, cw) * uf3
        for j in range(FFN_TAPS - 1):
            cf = cf + fw_ref[j:j + 1, cs].reshape(1, 1, cw) * uext_ref[:, lo + j:lo + j + tl, cs]
        gl = _gelu_tanh(cf.reshape(m, cw)) * vf
        f = f + _dot(gl.astype(BF16), wdn_ref[ci * cw:(ci + 1) * cw, :])
    y_ref[...] = _layer_norm(alpha * h + f, g2_ref[...], b2_ref[...])

    nffn_ref[...] = uext_ref[:, lo + tl:FFN_PAD + tl, :]
    if n_len > 1:
        @pl.when(s + 1 < n_len)
        def _():
            uext_ref[:, lo:FFN_PAD, :] = uext_ref[:, lo + tl:FFN_PAD + tl, :]


def _merge_ffn(x2, c_act, o_n, ctx, w_gate, w_br_a, w_br_b, w_o, g1, b1, w_up, fw, fb, w_down, g2, b2,
               *, nb, tl, n_chunk, alpha):
    n = ctx.shape[0]
    length = x2.shape[0] // n
    assert n % nb == 0 and length % tl == 0 and D_FF % (n_chunk * LANES) == 0
    n_seq, n_len = n // nb, length // tl
    assert n_len == 1 or (nb == 1 and tl >= FFN_TAPS - 1)
    m = nb * tl
    row_map = lambda i, s: (i * n_len + s, 0)
    seq_spec = pl.BlockSpec((nb, FFN_TAPS - 1, D_FF), lambda i, s: (i, 0, 0))
    return pl.pallas_call(
        functools.partial(_merge_ffn_kernel, nb=nb, tl=tl, n_len=n_len, n_chunk=n_chunk, alpha=alpha),
        grid=(n_seq, n_len),
        in_specs=[
            pl.BlockSpec((m, D_MODEL), row_map),
            pl.BlockSpec((m, CONV_CH), row_map),
            pl.BlockSpec((m, ATTN_WIDTH), row_map),
            seq_spec,
            _const_spec((D_MODEL, GATE_WIDTH)),
            _const_spec((CONV_CH, D_MODEL)),
            _const_spec((ATTN_WIDTH, D_MODEL)),
            _const_spec((D_MODEL, D_MODEL)),
            _const_spec((1, D_MODEL)), _const_spec((1, D_MODEL)),
            _const_spec((D_MODEL, 2 * D_FF)),
            _const_spec((FFN_TAPS, D_FF)),
            _const_spec((1, D_FF)),
            _const_spec((D_FF, D_MODEL)),
            _const_spec((1, D_MODEL)), _const_spec((1, D_MODEL)),
        ],
        out_specs=[pl.BlockSpec((m, D_MODEL), row_map), seq_spec],
        out_shape=[jax.ShapeDtypeStruct((n * length, D_MODEL), F32),
                   jax.ShapeDtypeStruct((n, FFN_TAPS - 1, D_FF), F32)],
        scratch_shapes=[pltpu.VMEM((nb, FFN_PAD + tl, D_FF), F32)],
        compiler_params=pltpu.CompilerParams(dimension_semantics=("arbitrary", "arbitrary"),
                                             vmem_limit_bytes=VMEM_LIMIT),
        name="merge_ffn",
    )(x2, c_act, o_n, ctx, w_gate, w_br_a, w_br_b, w_o, g1, b1, w_up, fw, fb, w_down, g2, b2)


def _rope_tables(pos):
    half = ROT_DIM // 2
    inv = ROPE_THETA ** (-jnp.arange(half, dtype=F32) / half)
    ang = pos.astype(F32)[:, None] * inv[None, :]
    cos, sin = jnp.cos(ang), jnp.sin(ang)
    n = pos.shape[0]
    rest = HEAD_DIM - ROT_DIM
    one_map = lambda a, b, fill: jnp.concatenate([a, b, jnp.full((n, rest), fill, F32)], axis=1)
    zeros = jnp.zeros_like(sin)
    per_lane = lambda t: jnp.concatenate([t] * (LANES // HEAD_DIM), axis=1)
    return (per_lane(one_map(cos, cos, 1.0)), per_lane(one_map(-sin, zeros, 0.0)),
            per_lane(one_map(zeros, sin, 0.0)))


def _tile_len(length, cap):
    t = min(length, cap)
    while length % t:
        t //= 2
    return t


def kernel(x_prompt, x_sample, cache_k, cache_v, state_conv, state_ffn, page_table, w_in, lambda_q1, lambda_k1,
           lambda_q2, lambda_k2, conv_dw_w, conv_dw_b, conv_ln_g, conv_ln_b, w_br_a, subln_g, w_br_b, w_o,
           ln1_g, ln1_b, w_up, ffn_dw_w, ffn_dw_b, w_down, ln2_g, ln2_b):
    b, s, _ = x_prompt.shape
    db, t, _ = x_sample.shape
    depth = w_in.shape[0]
    n_pages = page_table.shape[1]
    past = n_pages * PAGE_SIZE
    alpha = (2 * depth) ** 0.25
    n_phys = cache_k.shape[1]

    tabs_p = _rope_tables(jnp.arange(s, dtype=jnp.int32))
    tabs_s = tuple(jnp.tile(tb, (db, 1)) for tb in _rope_tables(past + jnp.arange(t, dtype=jnp.int32)))
    conv_zero = jnp.zeros((b, CONV_TAPS - 1, CONV_CH), F32)
    ffn_zero = jnp.zeros((b, FFN_TAPS - 1, D_FF), F32)

    tl_p = _tile_len(s, 256)
    bq = _tile_len(s, 512)
    grp = 8 if n_pages % 8 == 0 else 1
    nb_s = db
    while nb_s * t > 256 and nb_s % 2 == 0:
        nb_s //= 2

    yp = x_prompt.reshape(b * s, D_MODEL)
    ys = x_sample.reshape(db * t, D_MODEL)
    outs = [[] for _ in range(8)]
    row = lambda a: a.reshape(1, -1)
    for l in range(depth):
        lam_init = 0.8 - 0.6 * math.exp(-0.3 * l)
        lam = (jnp.exp(jnp.sum(lambda_q1[l] * lambda_k1[l])) - jnp.exp(jnp.sum(lambda_q2[l] * lambda_k2[l]))
               + lam_init).reshape(1).astype(F32)
        out_scale = 1.0 - lam_init
        w_qkvu = w_in[l][:, :QKVU_WIDTH].astype(BF16)
        w_gate = w_in[l][:, QKVU_WIDTH:].astype(BF16)
        conv_w = (conv_dw_w[l], row(conv_dw_b[l]), row(conv_ln_g[l]), row(conv_ln_b[l]))
        ffn_w = (w_gate, w_br_a[l].astype(BF16), w_br_b[l].astype(BF16), w_o[l].astype(BF16), row(ln1_g[l]),
                 row(ln1_b[l]), w_up[l].astype(BF16), ffn_dw_w[l], row(ffn_dw_b[l]), w_down[l].astype(BF16),
                 row(ln2_g[l]), row(ln2_b[l]))
        g_sub = row(subln_g[l])

        q, kf, vf, kb, vb, c_act, cp = _in_proj(yp, conv_zero, w_qkvu, *tabs_p, *conv_w, nb=1, tl=tl_p)
        o_n = _prompt_attn(q.reshape(b, s, QK_WIDTH), kb.reshape(b, s, QK_WIDTH), vb.reshape(b, s, ATTN_WIDTH),
                           lam, g_sub, out_scale, bq=bq, bk=bq)
        yp, fp = _merge_ffn(yp, c_act, o_n.reshape(b * s, ATTN_WIDTH), ffn_zero, *ffn_w,
                            nb=1, tl=tl_p, n_chunk=2, alpha=alpha)
        outs[0].append(kf.reshape(b, s, N_HEADS, 2, HEAD_DIM))
        outs[1].append(vf.reshape(b, s, N_HEADS, V_DIM))
        outs[2].append(cp)
        outs[3].append(fp)

        q, kf, vf, _, _, c_act, cs = _in_proj(ys, state_conv[l], w_qkvu, *tabs_s, *conv_w, nb=nb_s, tl=t)
        o_n = _decode_attn(q.reshape(db, t, QK_WIDTH), kf.reshape(db, t, QK_WIDTH), vf.reshape(db, t, ATTN_WIDTH),
                           cache_k[l].reshape(n_phys, PAGE_SIZE, QK_WIDTH),
                           cache_v[l].reshape(n_phys, PAGE_SIZE, ATTN_WIDTH),
                           page_table, lam, g_sub, out_scale, grp=grp)
        ys, fs = _merge_ffn(ys, c_act, o_n.reshape(db * t, ATTN_WIDTH), state_ffn[l], *ffn_w,
                            nb=nb_s, tl=t, n_chunk=2, alpha=alpha)
        outs[4].append(kf.reshape(db, t, N_HEADS, 2, HEAD_DIM))
        outs[5].append(vf.reshape(db, t, N_HEADS, V_DIM))
        outs[6].append(cs)
        outs[7].append(fs)

    stacked = [jnp.stack(o) for o in outs]
    return (yp.reshape(b, s, D_MODEL), ys.reshape(db, t, D_MODEL), *stacked)
```

```python
import functools
import math

import jax
import jax.numpy as jnp
import numpy as np
from jax import lax
from jax.experimental import pallas as pl
from jax.experimental.pallas import tpu as pltpu

D_MODEL = 1024
CONV_CH = D_MODEL // 2
CONV_TAPS = 31
N_HEADS = 4
HEAD_DIM = 64
V_DIM = 2 * HEAD_DIM
QK_WIDTH = N_HEADS * 2 * HEAD_DIM
ATTN_WIDTH = N_HEADS * V_DIM
ROT_DIM = HEAD_DIM // 4
ROPE_THETA = 500000.0
D_FF = 2816
FFN_TAPS = 3
LN_EPS = 1e-5
PAGE_SIZE = 128
QKVU_WIDTH = 2 * CONV_CH + 2 * QK_WIDTH + ATTN_WIDTH
GATE_WIDTH = 2 * D_MODEL

LANES = 128
SUBLANES = 8
VMEM_LIMIT = 56 * 1024 * 1024
CONV_PAD = 32
FFN_PAD = 8
NEG_BIG = -1e30
LOG2_E = math.log2(math.e)

BF16 = jnp.bfloat16
F32 = jnp.float32


def _const_spec(shape):
    nd = len(shape)
    return pl.BlockSpec(shape, lambda *_: (0,) * nd, pipeline_mode=pl.Buffered(1))


def _layer_norm(x, g, b):
    mu = jnp.mean(x, axis=-1, keepdims=True)
    xc = x - mu
    var = jnp.mean(xc * xc, axis=-1, keepdims=True)
    return xc * lax.rsqrt(var + LN_EPS) * g + b


def _dot(a, b):
    return jnp.dot(a, b, preferred_element_type=F32)


def _dot_nt(a, b):
    return lax.dot_general(a, b, (((1,), (1,)), ((), ())), preferred_element_type=F32)


def _in_proj_kernel(x_ref, w_ref, ctx_ref, cos_ref, sa_ref, sb_ref, dww_ref, dwb_ref, lng_ref, lnb_ref,
                    q_ref, kf_ref, vf_ref, c_ref, nconv_ref, *rest, nb, tl, n_len, k_transposed):
    if k_transposed:
        kb_ref, vb_ref, uext_ref = rest
    else:
        (uext_ref,) = rest
    s = pl.program_id(1)
    lo = CONV_PAD - (CONV_TAPS - 1)

    @pl.when(s == 0)
    def _():
        uext_ref[:, :lo, :] = jnp.zeros((nb, lo, CONV_CH), F32)
        uext_ref[:, lo:CONV_PAD, :] = ctx_ref[...]

    xb = x_ref[...].astype(BF16)
    proj = _dot(xb, w_ref[:, 2 * CONV_CH:])
    cos = cos_ref[...]
    sa = sa_ref[...]
    sb = sb_ref[...]
    half = ROT_DIM // 2
    q0 = 0
    k0 = q0 + QK_WIDTH
    v0 = k0 + QK_WIDTH
    scale = HEAD_DIM ** -0.5 * LOG2_E
    for h in range(N_HEADS):
        c0, c1 = h * LANES, (h + 1) * LANES
        t = proj[:, q0 + c0:q0 + c1]
        r = t * cos + pltpu.roll(t, LANES - half, axis=1) * sa + pltpu.roll(t, half, axis=1) * sb
        q_ref[:, c0:c1] = (r * scale).astype(BF16)
        t = proj[:, k0 + c0:k0 + c1]
        r = t * cos + pltpu.roll(t, LANES - half, axis=1) * sa + pltpu.roll(t, half, axis=1) * sb
        if k_transposed:
            rt = r.T
            kf_ref[c0:c1, :] = rt
            kb_ref[c0:c1, :] = rt.astype(BF16)
        else:
            kf_ref[:, c0:c1] = r
    v = proj[:, v0:v0 + ATTN_WIDTH]
    vf_ref[...] = v
    if k_transposed:
        vb_ref[...] = v.astype(BF16)

    glu = _dot(xb, w_ref[:, :2 * CONV_CH])
    u = glu[:, :CONV_CH] * jax.nn.sigmoid(glu[:, CONV_CH:2 * CONV_CH])
    uext_ref[:, CONV_PAD:CONV_PAD + tl, :] = u.reshape(nb, tl, CONV_CH)

    acc = jnp.broadcast_to(dwb_ref[...].reshape(1, 1, CONV_CH), (nb, tl, CONV_CH))
    for b in range(SUBLANES):
        span = tl if b == 0 else tl + SUBLANES
        part = None
        for a in range((CONV_PAD + SUBLANES - 1) // SUBLANES + 1):
            j = SUBLANES * a + b - lo
            if 0 <= j < CONV_TAPS:
                term = (dww_ref[j:j + 1, :].reshape(1, 1, CONV_CH)
                        * uext_ref[:, SUBLANES * a:SUBLANES * a + span, :])
                part = term if part is None else part + term
        acc = acc + part[:, b:b + tl, :]
    c = _layer_norm(acc.reshape(nb * tl, CONV_CH), lng_ref[...], lnb_ref[...])
    c_ref[...] = (c * jax.nn.sigmoid(c)).astype(BF16)

    nconv_ref[...] = uext_ref[:, lo + tl:CONV_PAD + tl, :]
    if n_len > 1:
        @pl.when(s + 1 < n_len)
        def _():
            uext_ref[:, lo:CONV_PAD, :] = uext_ref[:, lo + tl:CONV_PAD + tl, :]


def _in_proj(x2, ctx, w_qkvu, cos, sa, sb, dww, dwb, lng, lnb, *, nb, tl, k_transposed):
    n = ctx.shape[0]
    length = x2.shape[0] // n
    assert n % nb == 0 and length % tl == 0
    n_seq, n_len = n // nb, length // tl
    assert n_len == 1 or (nb == 1 and tl >= CONV_TAPS - 1)
    assert nb == 1 or not k_transposed
    m = nb * tl
    row_map = lambda i, s: (i * n_len + s, 0)
    tab_spec = pl.BlockSpec((m, LANES), lambda i, s: (s, 0))
    seq_spec = pl.BlockSpec((nb, CONV_TAPS - 1, CONV_CH), lambda i, s: (i, 0, 0))
    out_rows = lambda width: pl.BlockSpec((m, width), row_map)
    rows = n * length
    if k_transposed:
        k_spec = pl.BlockSpec((None, QK_WIDTH, tl), lambda i, s: (i, 0, s))
        k_shape = (n, QK_WIDTH, length)
        extra_specs = [k_spec, out_rows(ATTN_WIDTH)]
        extra_shapes = [jax.ShapeDtypeStruct(k_shape, BF16),
                        jax.ShapeDtypeStruct((rows, ATTN_WIDTH), BF16)]
    else:
        k_spec = out_rows(QK_WIDTH)
        k_shape = (rows, QK_WIDTH)
        extra_specs, extra_shapes = [], []
    return pl.pallas_call(
        functools.partial(_in_proj_kernel, nb=nb, tl=tl, n_len=n_len, k_transposed=k_transposed),
        grid=(n_seq, n_len),
        in_specs=[
            pl.BlockSpec((m, D_MODEL), row_map),
            _const_spec((D_MODEL, QKVU_WIDTH)),
            seq_spec,
            tab_spec, tab_spec, tab_spec,
            _const_spec((CONV_TAPS, CONV_CH)),
            _const_spec((1, CONV_CH)), _const_spec((1, CONV_CH)), _const_spec((1, CONV_CH)),
        ],
        out_specs=[out_rows(QK_WIDTH), k_spec, out_rows(ATTN_WIDTH), out_rows(CONV_CH), seq_spec] + extra_specs,
        out_shape=[
            jax.ShapeDtypeStruct((rows, QK_WIDTH), BF16),
            jax.ShapeDtypeStruct(k_shape, F32),
            jax.ShapeDtypeStruct((rows, ATTN_WIDTH), F32),
            jax.ShapeDtypeStruct((rows, CONV_CH), BF16),
            jax.ShapeDtypeStruct((n, CONV_TAPS - 1, CONV_CH), F32),
        ] + extra_shapes,
        scratch_shapes=[pltpu.VMEM((nb, CONV_PAD + tl, CONV_CH), F32)],
        compiler_params=pltpu.CompilerParams(dimension_semantics=("arbitrary", "arbitrary"),
                                             vmem_limit_bytes=VMEM_LIMIT),
        name="in_proj",
    )(x2, w_qkvu, ctx, cos, sa, sb, dww, dwb, lng, lnb)


def _subln(o1, l1, o2, l2, lam, g, out_scale):
    o = o1 / l1 - lam * (o2 / l2)
    ms = jnp.mean(o * o, axis=-1, keepdims=True)
    return o * lax.rsqrt(ms + LN_EPS) * g * out_scale


def _online_softmax_step(s, m_prev, l_prev):
    m_next = jnp.maximum(m_prev, jnp.max(s, axis=1, keepdims=True))
    p = jnp.exp2(s - m_next[:, :1])
    alpha = jnp.exp2(m_prev - m_next)
    l_next = alpha * l_prev + jnp.sum(p, axis=1, keepdims=True)
    return p, alpha, m_next, l_next


def _prompt_attn_kernel(qi_ref, ki_ref, lam_ref, q_ref, k_ref, v_ref, g_ref, o_ref,
                        qz_ref, m_ref, l_ref, acc_ref, s_ref, p_ref, a_ref, *, bq, bk, rc, out_scale):
    p_idx = pl.program_id(1)
    qi = qi_ref[p_idx]
    ki = ki_ref[p_idx]
    last_ki = (qi * bq + bq - 1) // bk
    n_tiles = bk // LANES

    @pl.when(ki == 0)
    def _():
        lane = lax.broadcasted_iota(jnp.int32, (bq, LANES), 1)
        for h in range(N_HEADS):
            qh = q_ref[:, h * LANES:(h + 1) * LANES]
            qz_ref[h, :bq, :] = jnp.where(lane < HEAD_DIM, qh, jnp.zeros_like(qh))
            qz_ref[h, bq:, :] = jnp.where(lane >= HEAD_DIM, qh, jnp.zeros_like(qh))
        m_ref[...] = jnp.full(m_ref.shape, NEG_BIG, F32)
        l_ref[...] = jnp.zeros(l_ref.shape, F32)
        acc_ref[...] = jnp.zeros(acc_ref.shape, F32)

    def body(masked):
        if masked:
            rel = lax.broadcasted_iota(jnp.int32, (rc, bk), 1) - lax.broadcasted_iota(jnp.int32, (rc, bk), 0)
            base = qi * bq - ki * bk
        for h in range(N_HEADS):
            s_ref[...] = _dot(qz_ref[h], k_ref[h * LANES:(h + 1) * LANES, :])
            for c in range(2 * bq // rc):
                rows = slice(c * rc, (c + 1) * rc)
                sc = s_ref[rows, :]
                if masked:
                    sc = jnp.where(rel <= base + (c * rc) % bq, sc, NEG_BIG)
                tile_max = sc[:, :LANES]
                for j in range(1, n_tiles):
                    tile_max = jnp.maximum(tile_max, sc[:, j * LANES:(j + 1) * LANES])
                m_prev = m_ref[h, rows, :]
                m_next = jnp.maximum(m_prev, jnp.max(tile_max, axis=1, keepdims=True))
                p = jnp.exp2(sc - jnp.concatenate([m_next] * n_tiles, axis=1))
                alpha = jnp.exp2(m_prev - m_next)
                tile_sum = p[:, :LANES]
                for j in range(1, n_tiles):
                    tile_sum = tile_sum + p[:, j * LANES:(j + 1) * LANES]
                l_ref[h, rows, :] = alpha * l_ref[h, rows, :] + tile_sum
                m_ref[h, rows, :] = m_next
                a_ref[rows, :] = alpha
                p_ref[rows, :] = p.astype(BF16)
            acc_ref[h] = a_ref[...] * acc_ref[h] + _dot(p_ref[...], v_ref[:, h * LANES:(h + 1) * LANES])

    on_diag = ki * bk + bk - 1 > qi * bq

    @pl.when(on_diag)
    def _():
        body(True)

    @pl.when(jnp.logical_not(on_diag))
    def _():
        body(False)

    @pl.when(ki == last_ki)
    def _():
        lam = lam_ref[0]
        for h in range(N_HEADS):
            l1 = jnp.sum(l_ref[h, :bq, :], axis=1, keepdims=True)
            l2 = jnp.sum(l_ref[h, bq:, :], axis=1, keepdims=True)
            o = _subln(acc_ref[h, :bq, :], l1, acc_ref[h, bq:, :], l2, lam, g_ref[...], out_scale)
            o_ref[:, h * LANES:(h + 1) * LANES] = o.astype(BF16)


def _prompt_attn(q, kt, v, lam, subln_g, out_scale, *, bq, bk, rc):
    b, s, _ = q.shape
    assert s % bq == 0 and s % bk == 0 and bq % rc == 0
    pairs = [(i, j) for i in range(s // bq) for j in range((i * bq + bq - 1) // bk + 1)]
    qi_tab = jnp.asarray(np.array([p[0] for p in pairs], np.int32))
    ki_tab = jnp.asarray(np.array([p[1] for p in pairs], np.int32))
    q_spec = pl.BlockSpec((None, bq, QK_WIDTH), lambda bi, p, qt, kt: (bi, qt[p], 0))
    k_spec = pl.BlockSpec((None, QK_WIDTH, bk), lambda bi, p, qt, kt: (bi, 0, kt[p]))
    v_spec = pl.BlockSpec((None, bk, ATTN_WIDTH), lambda bi, p, qt, kt: (bi, kt[p], 0))
    grid_spec = pltpu.PrefetchScalarGridSpec(
        num_scalar_prefetch=2,
        grid=(b, len(pairs)),
        in_specs=[
            pl.BlockSpec(memory_space=pltpu.SMEM),
            q_spec, k_spec, v_spec,
            pl.BlockSpec((1, V_DIM), lambda bi, p, qt, kt: (0, 0)),
        ],
        out_specs=q_spec,
        scratch_shapes=[
            pltpu.VMEM((N_HEADS, 2 * bq, LANES), BF16),
            pltpu.VMEM((N_HEADS, 2 * bq, LANES), F32),
            pltpu.VMEM((N_HEADS, 2 * bq, LANES), F32),
            pltpu.VMEM((N_HEADS, 2 * bq, V_DIM), F32),
            pltpu.VMEM((2 * bq, bk), F32),
            pltpu.VMEM((2 * bq, bk), BF16),
            pltpu.VMEM((2 * bq, LANES), F32),
        ],
    )
    return pl.pallas_call(
        functools.partial(_prompt_attn_kernel, bq=bq, bk=bk, rc=rc, out_scale=out_scale),
        grid_spec=grid_spec,
        out_shape=jax.ShapeDtypeStruct((b, s, ATTN_WIDTH), BF16),
        compiler_params=pltpu.CompilerParams(dimension_semantics=("arbitrary", "arbitrary"),
                                             vmem_limit_bytes=VMEM_LIMIT),
        name="prompt_attn",
    )(qi_tab, ki_tab, lam, q, kt, v, subln_g)


def _decode_attn_kernel(pt_ref, lam_ref, q_ref, kn_ref, vn_ref, g_ref, e_ref, hm_ref, *refs,
                        n_grp, grp, t_new, out_scale):
    k_refs = refs[:grp]
    v_refs = refs[grp:2 * grp]
    o_ref, qbd_ref, m_ref, l_ref, acc_ref = refs[2 * grp:]
    g = pl.program_id(1)
    n_rows = 2 * N_HEADS * t_new
    rows_per_head = 2 * t_new

    @pl.when(g == 0)
    def _():
        qf = q_ref[...].astype(F32)
        lane = lax.broadcasted_iota(jnp.int32, (t_new, QK_WIDTH), 1)
        parts = [jnp.where((lane >= r * HEAD_DIM) & (lane < (r + 1) * HEAD_DIM), qf, 0.0)
                 for r in range(2 * N_HEADS)]
        qbd_ref[...] = jnp.concatenate(parts, axis=0).astype(BF16)
        m_ref[...] = jnp.full(m_ref.shape, NEG_BIG, F32)
        l_ref[...] = jnp.zeros(l_ref.shape, F32)
        acc_ref[...] = jnp.zeros(acc_ref.shape, F32)

    def softmax_update(s):
        p, alpha, m_next, l_next = _online_softmax_step(s, m_ref[...], l_ref[...])
        m_ref[...] = m_next
        l_ref[...] = l_next
        return p.astype(BF16), alpha

    kt = jnp.concatenate([r[...].astype(BF16) for r in k_refs], axis=1)
    pb, alpha = softmax_update(_dot(qbd_ref[...], kt))
    p_rows = jnp.concatenate([pb[:, j * PAGE_SIZE:(j + 1) * PAGE_SIZE] for j in range(grp)], axis=0)
    p4 = _dot(p_rows, e_ref[...])
    hm = hm_ref[...]
    p4 = jnp.concatenate([(p4[j * n_rows:(j + 1) * n_rows, :] * hm).astype(BF16) for j in range(grp)], axis=1)
    vm = jnp.concatenate([r[...].astype(BF16) for r in v_refs], axis=0)
    acc_ref[...] = alpha * acc_ref[...] + _dot(p4, vm)

    @pl.when(g == n_grp - 1)
    def _():
        pad = jnp.zeros((LANES - t_new, QK_WIDTH), F32)
        kn = jnp.concatenate([kn_ref[...], pad], axis=0).astype(BF16)
        vn = jnp.concatenate([vn_ref[...], pad], axis=0).astype(BF16)
        row_t = lax.broadcasted_iota(jnp.int32, (n_rows, LANES), 0) % t_new
        col = lax.broadcasted_iota(jnp.int32, (n_rows, LANES), 1)
        s = jnp.where(col <= row_t, _dot_nt(qbd_ref[...], kn), NEG_BIG)
        pn, alpha_n = softmax_update(s)
        pv = _dot(pn, vn)
        pv = jnp.concatenate([pv[h * rows_per_head:(h + 1) * rows_per_head, h * LANES:(h + 1) * LANES]
                              for h in range(N_HEADS)], axis=0)
        acc = alpha_n * acc_ref[...] + pv
        lsum = l_ref[...]
        lam = lam_ref[0]
        for h in range(N_HEADS):
            r1 = slice(h * rows_per_head, h * rows_per_head + t_new)
            r2 = slice(h * rows_per_head + t_new, (h + 1) * rows_per_head)
            o = _subln(acc[r1], lsum[r1], acc[r2], lsum[r2], lam, g_ref[...], out_scale)
            o_ref[:, h * LANES:(h + 1) * LANES] = o.astype(BF16)


def _decode_attn(q, k_new, v_new, cache_kt, cache_vm, page_table, lam, subln_g, out_scale, *, grp):
    n, t_new, _ = q.shape
    n_pages = page_table.shape[1]
    assert n_pages % grp == 0
    n_grp = n_pages // grp
    n_rows = 2 * N_HEADS * t_new
    tok_spec = pl.BlockSpec((None, t_new, QK_WIDTH), lambda i, g, pt: (i, 0, 0))
    vrows = PAGE_SIZE * N_HEADS
    col = np.arange(vrows)
    expand = jnp.asarray(col[None, :] // N_HEADS == np.arange(PAGE_SIZE)[:, None], BF16)
    head_mask = jnp.asarray(col[None, :] % N_HEADS == (np.arange(n_rows) // (2 * t_new))[:, None], F32)

    def kpage_spec(j):
        return pl.BlockSpec((None, QK_WIDTH, PAGE_SIZE), lambda i, g, pt: (pt[i, g * grp + j], 0, 0))

    def vpage_spec(j):
        return pl.BlockSpec((None, vrows, V_DIM), lambda i, g, pt: (pt[i, g * grp + j], 0, 0))

    grid_spec = pltpu.PrefetchScalarGridSpec(
        num_scalar_prefetch=1,
        grid=(n, n_grp),
        in_specs=[pl.BlockSpec(memory_space=pltpu.SMEM), tok_spec, tok_spec, tok_spec,
                  pl.BlockSpec((1, V_DIM), lambda i, g, pt: (0, 0)),
                  pl.BlockSpec((PAGE_SIZE, vrows), lambda i, g, pt: (0, 0)),
                  pl.BlockSpec((n_rows, vrows), lambda i, g, pt: (0, 0))]
                 + [kpage_spec(j) for j in range(grp)] + [vpage_spec(j) for j in range(grp)],
        out_specs=tok_spec,
        scratch_shapes=[
            pltpu.VMEM((n_rows, QK_WIDTH), BF16),
            pltpu.VMEM((n_rows, LANES), F32),
            pltpu.VMEM((n_rows, LANES), F32),
            pltpu.VMEM((n_rows, V_DIM), F32),
        ],
    )
    return pl.pallas_call(
        functools.partial(_decode_attn_kernel, n_grp=n_grp, grp=grp, t_new=t_new, out_scale=out_scale),
        grid_spec=grid_spec,
        out_shape=jax.ShapeDtypeStruct((n, t_new, ATTN_WIDTH), BF16),
        compiler_params=pltpu.CompilerParams(dimension_semantics=("arbitrary", "arbitrary"),
                                             vmem_limit_bytes=VMEM_LIMIT),
        name="decode_attn",
    )(page_table, lam, q, k_new, v_new, subln_g, expand, head_mask, *([cache_kt] * grp), *([cache_vm] * grp))


def _gelu_tanh(x):
    return 0.5 * x * (1.0 + jnp.tanh(math.sqrt(2.0 / math.pi) * (x + 0.044715 * (x * x * x))))


def _merge_ffn_kernel(x_ref, c_ref, o_ref, ctx_ref, wg_ref, wa_ref, wb_ref, wo_ref, g1_ref, b1_ref,
                      wup_ref, fw_ref, fb_ref, wdn_ref, g2_ref, b2_ref,
                      y_ref, nffn_ref, uext_ref, *, nb, tl, n_len, n_chunk, alpha):
    s = pl.program_id(1)
    lo = FFN_PAD - (FFN_TAPS - 1)
    m = nb * tl

    @pl.when(s == 0)
    def _():
        uext_ref[:, lo:FFN_PAD, :] = ctx_ref[...]

    x = x_ref[...]
    gates = _dot(x.astype(BF16), wg_ref[...])
    mixed = (jax.nn.sigmoid(gates[:, :D_MODEL]) * _dot(c_ref[...], wa_ref[...])
             + jax.nn.sigmoid(gates[:, D_MODEL:]) * _dot(o_ref[...], wb_ref[...]))
    h = _layer_norm(alpha * x + _dot(mixed.astype(BF16), wo_ref[...]), g1_ref[...], b1_ref[...])
    hb = h.astype(BF16)

    cw = D_FF // n_chunk
    f = jnp.zeros((m, D_MODEL), F32)
    for ci in range(n_chunk):
        cs = slice(ci * cw, (ci + 1) * cw)
        uf = _dot(hb, wup_ref[:, ci * cw:(ci + 1) * cw])
        vf = _dot(hb, wup_ref[:, D_FF + ci * cw:D_FF + (ci + 1) * cw])
        uf3 = uf.reshape(nb, tl, cw)
        uext_ref[:, FFN_PAD:FFN_PAD + tl, cs] = uf3
        cf = fb_ref[:, cs].reshape(1, 1, cw) + fw_ref[FFN_TAPS - 1:FFN_TAPS, cs].reshape(1, 1, cw) * uf3
        for j in range(FFN_TAPS - 1):
            cf = cf + fw_ref[j:j + 1, cs].reshape(1, 1, cw) * uext_ref[:, lo + j:lo + j + tl, cs]
        gl = _gelu_tanh(cf.reshape(m, cw)) * vf
        f = f + _dot(gl.astype(BF16), wdn_ref[ci * cw:(ci + 1) * cw, :])
    y_ref[...] = _layer_norm(alpha * h + f, g2_ref[...], b2_ref[...])

    nffn_ref[...] = uext_ref[:, lo + tl:FFN_PAD + tl, :]
    if n_len > 1:
        @pl.when(s + 1 < n_len)
        def _():
            uext_ref[:, lo:FFN_PAD, :] = uext_ref[:, lo + tl:FFN_PAD + tl, :]


def _merge_ffn(x2, c_act, o_n, ctx, w_gate, w_br_a, w_br_b, w_o, g1, b1, w_up, fw, fb, w_down, g2, b2,
               *, nb, tl, n_chunk, alpha):
    n = ctx.shape[0]
    length = x2.shape[0] // n
    assert n % nb == 0 and length % tl == 0 and D_FF % (n_chunk * LANES) == 0
    n_seq, n_len = n // nb, length // tl
    assert n_len == 1 or (nb == 1 and tl >= FFN_TAPS - 1)
    m = nb * tl
    row_map = lambda i, s: (i * n_len + s, 0)
    seq_spec = pl.BlockSpec((nb, FFN_TAPS - 1, D_FF), lambda i, s: (i, 0, 0))
    return pl.pallas_call(
        functools.partial(_merge_ffn_kernel, nb=nb, tl=tl, n_len=n_len, n_chunk=n_chunk, alpha=alpha),
        grid=(n_seq, n_len),
        in_specs=[
            pl.BlockSpec((m, D_MODEL), row_map),
            pl.BlockSpec((m, CONV_CH), row_map),
            pl.BlockSpec((m, ATTN_WIDTH), row_map),
            seq_spec,
            _const_spec((D_MODEL, GATE_WIDTH)),
            _const_spec((CONV_CH, D_MODEL)),
            _const_spec((ATTN_WIDTH, D_MODEL)),
            _const_spec((D_MODEL, D_MODEL)),
            _const_spec((1, D_MODEL)), _const_spec((1, D_MODEL)),
            _const_spec((D_MODEL, 2 * D_FF)),
            _const_spec((FFN_TAPS, D_FF)),
            _const_spec((1, D_FF)),
            _const_spec((D_FF, D_MODEL)),
            _const_spec((1, D_MODEL)), _const_spec((1, D_MODEL)),
        ],
        out_specs=[pl.BlockSpec((m, D_MODEL), row_map), seq_spec],
        out_shape=[jax.ShapeDtypeStruct((n * length, D_MODEL), F32),
                   jax.ShapeDtypeStruct((n, FFN_TAPS - 1, D_FF), F32)],
        scratch_shapes=[pltpu.VMEM((nb, FFN_PAD + tl, D_FF), F32)],
        compiler_params=pltpu.CompilerParams(dimension_semantics=("arbitrary", "arbitrary"),
                                             vmem_limit_bytes=VMEM_LIMIT),
        name="merge_ffn",
    )(x2, c_act, o_n, ctx, w_gate, w_br_a, w_br_b, w_o, g1, b1, w_up, fw, fb, w_down, g2, b2)


def _rope_tables(pos):
    half = ROT_DIM // 2
    inv = ROPE_THETA ** (-jnp.arange(half, dtype=F32) / half)
    ang = pos.astype(F32)[:, None] * inv[None, :]
    cos, sin = jnp.cos(ang), jnp.sin(ang)
    n = pos.shape[0]
    rest = HEAD_DIM - ROT_DIM
    one_map = lambda a, b, fill: jnp.concatenate([a, b, jnp.full((n, rest), fill, F32)], axis=1)
    zeros = jnp.zeros_like(sin)
    per_lane = lambda t: jnp.concatenate([t] * (LANES // HEAD_DIM), axis=1)
    return (per_lane(one_map(cos, cos, 1.0)), per_lane(one_map(-sin, zeros, 0.0)),
            per_lane(one_map(zeros, sin, 0.0)))


def _tile_len(length, cap):
    t = min(length, cap)
    while length % t:
        t //= 2
    return t


def kernel(x_prompt, x_sample, cache_k, cache_v, state_conv, state_ffn, page_table, w_in, lambda_q1, lambda_k1,
           lambda_q2, lambda_k2, conv_dw_w, conv_dw_b, conv_ln_g, conv_ln_b, w_br_a, subln_g, w_br_b, w_o,
           ln1_g, ln1_b, w_up, ffn_dw_w, ffn_dw_b, w_down, ln2_g, ln2_b):
    b, s, _ = x_prompt.shape
    db, t, _ = x_sample.shape
    depth = w_in.shape[0]
    n_pages = page_table.shape[1]
    past = n_pages * PAGE_SIZE
    alpha = (2 * depth) ** 0.25
    n_phys = cache_k.shape[1]

    tabs_p = _rope_tables(jnp.arange(s, dtype=jnp.int32))
    tabs_s = tuple(jnp.tile(tb, (db, 1)) for tb in _rope_tables(past + jnp.arange(t, dtype=jnp.int32)))
    conv_zero = jnp.zeros((b, CONV_TAPS - 1, CONV_CH), F32)
    ffn_zero = jnp.zeros((b, FFN_TAPS - 1, D_FF), F32)

    tl_p = _tile_len(s, 256)
    bq = _tile_len(s, 512)
    grp = 8 if n_pages % 8 == 0 else 1
    nb_s = db
    while nb_s * t > 256 and nb_s % 2 == 0:
        nb_s //= 2

    yp = x_prompt.reshape(b * s, D_MODEL)
    ys = x_sample.reshape(db * t, D_MODEL)
    outs = [[] for _ in range(8)]
    row = lambda a: a.reshape(1, -1)
    for l in range(depth):
        lam_init = 0.8 - 0.6 * math.exp(-0.3 * l)
        lam = (jnp.exp(jnp.sum(lambda_q1[l] * lambda_k1[l])) - jnp.exp(jnp.sum(lambda_q2[l] * lambda_k2[l]))
               + lam_init).reshape(1).astype(F32)
        out_scale = 1.0 - lam_init
        w_qkvu = w_in[l][:, :QKVU_WIDTH].astype(BF16)
        w_gate = w_in[l][:, QKVU_WIDTH:].astype(BF16)
        conv_w = (conv_dw_w[l], row(conv_dw_b[l]), row(conv_ln_g[l]), row(conv_ln_b[l]))
        ffn_w = (w_gate, w_br_a[l].astype(BF16), w_br_b[l].astype(BF16), w_o[l].astype(BF16), row(ln1_g[l]),
                 row(ln1_b[l]), w_up[l].astype(BF16), ffn_dw_w[l], row(ffn_dw_b[l]), w_down[l].astype(BF16),
                 row(ln2_g[l]), row(ln2_b[l]))
        g_sub = row(subln_g[l])

        q, kt, vf, c_act, cp, ktb, vb = _in_proj(yp, conv_zero, w_qkvu, *tabs_p, *conv_w, nb=1, tl=tl_p,
                                                 k_transposed=True)
        o_n = _prompt_attn(q.reshape(b, s, QK_WIDTH), ktb, vb.reshape(b, s, ATTN_WIDTH),
                           lam, g_sub, out_scale, bq=bq, bk=bq, rc=min(bq, 32))
        yp, fp = _merge_ffn(yp, c_act, o_n.reshape(b * s, ATTN_WIDTH), ffn_zero, *ffn_w,
                            nb=1, tl=tl_p, n_chunk=2, alpha=alpha)
        outs[0].append(jnp.transpose(kt.reshape(b, N_HEADS, 2, HEAD_DIM, s), (0, 4, 1, 2, 3)))
        outs[1].append(vf.reshape(b, s, N_HEADS, V_DIM))
        outs[2].append(cp)
        outs[3].append(fp)

        q, kf, vf, c_act, cs = _in_proj(ys, state_conv[l], w_qkvu, *tabs_s, *conv_w, nb=nb_s, tl=t,
                                        k_transposed=False)
        cache_kt = jnp.transpose(cache_k[l], (0, 2, 3, 4, 1)).reshape(n_phys, QK_WIDTH, PAGE_SIZE)
        cache_vm = cache_v[l].reshape(n_phys, PAGE_SIZE * N_HEADS, V_DIM)
        o_n = _decode_attn(q.reshape(db, t, QK_WIDTH), kf.reshape(db, t, QK_WIDTH), vf.reshape(db, t, ATTN_WIDTH),
                           cache_kt, cache_vm, page_table, lam, g_sub, out_scale, grp=grp)
        ys, fs = _merge_ffn(ys, c_act, o_n.reshape(db * t, ATTN_WIDTH), state_ffn[l], *ffn_w,
                            nb=nb_s, tl=t, n_chunk=2, alpha=alpha)
        outs[4].append(kf.reshape(db, t, N_HEADS, 2, HEAD_DIM))
        outs[5].append(vf.reshape(db, t, N_HEADS, V_DIM))
        outs[6].append(cs)
        outs[7].append(fs)

    stacked = [jnp.stack(o) for o in outs]
    return (yp.reshape(b, s, D_MODEL), ys.reshape(db, t, D_MODEL), *stacked)
```

```python
import functools
import math

import jax
import jax.numpy as jnp
import numpy as np
from jax import lax
from jax.experimental import pallas as pl
from jax.experimental.pallas import tpu as pltpu

D_MODEL = 1024
CONV_CH = D_MODEL // 2
CONV_TAPS = 31
N_HEADS = 4
HEAD_DIM = 64
V_DIM = 2 * HEAD_DIM
QK_WIDTH = N_HEADS * 2 * HEAD_DIM
ATTN_WIDTH = N_HEADS * V_DIM
ROT_DIM = HEAD_DIM // 4
ROPE_THETA = 500000.0
D_FF = 2816
FFN_TAPS = 3
LN_EPS = 1e-5
PAGE_SIZE = 128
QKVU_WIDTH = 2 * CONV_CH + 2 * QK_WIDTH + ATTN_WIDTH
GATE_WIDTH = 2 * D_MODEL

LANES = 128
SUBLANES = 8
VMEM_LIMIT = 56 * 1024 * 1024
CONV_PAD = 32
FFN_PAD = 8
NEG_BIG = -1e30
LOG2_E = math.log2(math.e)

BF16 = jnp.bfloat16
F32 = jnp.float32


def _const_spec(shape):
    nd = len(shape)
    return pl.BlockSpec(shape, lambda *_: (0,) * nd, pipeline_mode=pl.Buffered(1))


def _layer_norm(x, g, b):
    mu = jnp.mean(x, axis=-1, keepdims=True)
    xc = x - mu
    var = jnp.mean(xc * xc, axis=-1, keepdims=True)
    return xc * lax.rsqrt(var + LN_EPS) * g + b


def _dot(a, b):
    return jnp.dot(a, b, preferred_element_type=F32)


def _dot_nt(a, b):
    return lax.dot_general(a, b, (((1,), (1,)), ((), ())), preferred_element_type=F32)


def _in_proj_kernel(x_ref, w_ref, ctx_ref, cos_ref, sa_ref, sb_ref, dww_ref, dwb_ref, lng_ref, lnb_ref,
                    q_ref, kf_ref, vf_ref, c_ref, nconv_ref, *rest, nb, tl, n_len, k_transposed):
    if k_transposed:
        kb_ref, vb_ref, uext_ref = rest
    else:
        (uext_ref,) = rest
    s = pl.program_id(1)
    lo = CONV_PAD - (CONV_TAPS - 1)

    @pl.when(s == 0)
    def _():
        uext_ref[:, :lo, :] = jnp.zeros((nb, lo, CONV_CH), F32)
        uext_ref[:, lo:CONV_PAD, :] = ctx_ref[...]

    xb = x_ref[...].astype(BF16)
    proj = _dot(xb, w_ref[:, 2 * CONV_CH:])
    cos = cos_ref[...]
    sa = sa_ref[...]
    sb = sb_ref[...]
    half = ROT_DIM // 2
    q0 = 0
    k0 = q0 + QK_WIDTH
    v0 = k0 + QK_WIDTH
    scale = HEAD_DIM ** -0.5 * LOG2_E
    for h in range(N_HEADS):
        c0, c1 = h * LANES, (h + 1) * LANES
        t = proj[:, q0 + c0:q0 + c1]
        r = t * cos + pltpu.roll(t, LANES - half, axis=1) * sa + pltpu.roll(t, half, axis=1) * sb
        q_ref[:, c0:c1] = (r * scale).astype(BF16)
        t = proj[:, k0 + c0:k0 + c1]
        r = t * cos + pltpu.roll(t, LANES - half, axis=1) * sa + pltpu.roll(t, half, axis=1) * sb
        if k_transposed:
            rt = r.T
            kf_ref[c0:c1, :] = rt
            kb_ref[c0:c1, :] = rt.astype(BF16)
        else:
            kf_ref[:, c0:c1] = r
    v = proj[:, v0:v0 + ATTN_WIDTH]
    if k_transposed:
        for h in range(N_HEADS):
            vf_ref[pl.ds(h, nb * tl, stride=N_HEADS), :] = v[:, h * LANES:(h + 1) * LANES]
        vb_ref[...] = v.astype(BF16)
    else:
        vf_ref[...] = v

    glu = _dot(xb, w_ref[:, :2 * CONV_CH])
    u = glu[:, :CONV_CH] * jax.nn.sigmoid(glu[:, CONV_CH:2 * CONV_CH])
    uext_ref[:, CONV_PAD:CONV_PAD + tl, :] = u.reshape(nb, tl, CONV_CH)

    acc = jnp.broadcast_to(dwb_ref[...].reshape(1, 1, CONV_CH), (nb, tl, CONV_CH))
    for b in range(SUBLANES):
        span = tl if b == 0 else tl + SUBLANES
        part = None
        for a in range((CONV_PAD + SUBLANES - 1) // SUBLANES + 1):
            j = SUBLANES * a + b - lo
            if 0 <= j < CONV_TAPS:
                term = (dww_ref[j:j + 1, :].reshape(1, 1, CONV_CH)
                        * uext_ref[:, SUBLANES * a:SUBLANES * a + span, :])
                part = term if part is None else part + term
        acc = acc + part[:, b:b + tl, :]
    c = _layer_norm(acc.reshape(nb * tl, CONV_CH), lng_ref[...], lnb_ref[...])
    c_ref[...] = (c * jax.nn.sigmoid(c)).astype(BF16)

    nconv_ref[...] = uext_ref[:, lo + tl:CONV_PAD + tl, :]
    if n_len > 1:
        @pl.when(s + 1 < n_len)
        def _():
            uext_ref[:, lo:CONV_PAD, :] = uext_ref[:, lo + tl:CONV_PAD + tl, :]


def _in_proj(x2, ctx, w_qkvu, cos, sa, sb, dww, dwb, lng, lnb, *, nb, tl, k_transposed):
    n = ctx.shape[0]
    length = x2.shape[0] // n
    assert n % nb == 0 and length % tl == 0
    n_seq, n_len = n // nb, length // tl
    assert n_len == 1 or (nb == 1 and tl >= CONV_TAPS - 1)
    assert nb == 1 or not k_transposed
    m = nb * tl
    row_map = lambda i, s: (i * n_len + s, 0)
    tab_spec = pl.BlockSpec((m, LANES), lambda i, s: (s, 0))
    seq_spec = pl.BlockSpec((nb, CONV_TAPS - 1, CONV_CH), lambda i, s: (i, 0, 0))
    out_rows = lambda width: pl.BlockSpec((m, width), row_map)
    rows = n * length
    if k_transposed:
        k_spec = pl.BlockSpec((None, QK_WIDTH, tl), lambda i, s: (i, 0, s))
        k_shape = (n, QK_WIDTH, length)
        v_spec = pl.BlockSpec((m * N_HEADS, V_DIM), row_map)
        v_shape = (rows * N_HEADS, V_DIM)
        extra_specs = [k_spec, out_rows(ATTN_WIDTH)]
        extra_shapes = [jax.ShapeDtypeStruct(k_shape, BF16),
                        jax.ShapeDtypeStruct((rows, ATTN_WIDTH), BF16)]
    else:
        k_spec = out_rows(QK_WIDTH)
        k_shape = (rows, QK_WIDTH)
        v_spec = out_rows(ATTN_WIDTH)
        v_shape = (rows, ATTN_WIDTH)
        extra_specs, extra_shapes = [], []
    return pl.pallas_call(
        functools.partial(_in_proj_kernel, nb=nb, tl=tl, n_len=n_len, k_transposed=k_transposed),
        grid=(n_seq, n_len),
        in_specs=[
            pl.BlockSpec((m, D_MODEL), row_map),
            _const_spec((D_MODEL, QKVU_WIDTH)),
            seq_spec,
            tab_spec, tab_spec, tab_spec,
            _const_spec((CONV_TAPS, CONV_CH)),
            _const_spec((1, CONV_CH)), _const_spec((1, CONV_CH)), _const_spec((1, CONV_CH)),
        ],
        out_specs=[out_rows(QK_WIDTH), k_spec, v_spec, out_rows(CONV_CH), seq_spec] + extra_specs,
        out_shape=[
            jax.ShapeDtypeStruct((rows, QK_WIDTH), BF16),
            jax.ShapeDtypeStruct(k_shape, F32),
            jax.ShapeDtypeStruct(v_shape, F32),
            jax.ShapeDtypeStruct((rows, CONV_CH), BF16),
            jax.ShapeDtypeStruct((n, CONV_TAPS - 1, CONV_CH), F32),
        ] + extra_shapes,
        scratch_shapes=[pltpu.VMEM((nb, CONV_PAD + tl, CONV_CH), F32)],
        compiler_params=pltpu.CompilerParams(dimension_semantics=("arbitrary", "arbitrary"),
                                             vmem_limit_bytes=VMEM_LIMIT),
        name="in_proj",
    )(x2, w_qkvu, ctx, cos, sa, sb, dww, dwb, lng, lnb)


def _subln(o1, l1, o2, l2, lam, g, out_scale):
    o = o1 / l1 - lam * (o2 / l2)
    ms = jnp.mean(o * o, axis=-1, keepdims=True)
    return o * lax.rsqrt(ms + LN_EPS) * g * out_scale


def _online_softmax_step(s, m_prev, l_prev):
    m_next = jnp.maximum(m_prev, jnp.max(s, axis=1, keepdims=True))
    p = jnp.exp2(s - m_next[:, :1])
    alpha = jnp.exp2(m_prev - m_next)
    l_next = alpha * l_prev + jnp.sum(p, axis=1, keepdims=True)
    return p, alpha, m_next, l_next


def _prompt_attn_kernel(qi_ref, ki_ref, lam_ref, q_ref, k_ref, v_ref, g_ref, o_ref,
                        qz_ref, m_ref, l_ref, acc_ref, s_ref, p_ref, a_ref, *, bq, bk, rc, out_scale):
    p_idx = pl.program_id(1)
    qi = qi_ref[p_idx]
    ki = ki_ref[p_idx]
    last_ki = (qi * bq + bq - 1) // bk
    n_tiles = bk // LANES

    @pl.when(ki == 0)
    def _():
        lane = lax.broadcasted_iota(jnp.int32, (bq, LANES), 1)
        for h in range(N_HEADS):
            qh = q_ref[:, h * LANES:(h + 1) * LANES]
            qz_ref[h, :bq, :] = jnp.where(lane < HEAD_DIM, qh, jnp.zeros_like(qh))
            qz_ref[h, bq:, :] = jnp.where(lane >= HEAD_DIM, qh, jnp.zeros_like(qh))
        m_ref[...] = jnp.full(m_ref.shape, NEG_BIG, F32)
        l_ref[...] = jnp.zeros(l_ref.shape, F32)
        acc_ref[...] = jnp.zeros(acc_ref.shape, F32)

    def body(masked):
        if masked:
            rel = lax.broadcasted_iota(jnp.int32, (rc, bk), 1) - lax.broadcasted_iota(jnp.int32, (rc, bk), 0)
            base = qi * bq - ki * bk
        for h in range(N_HEADS):
            s_ref[...] = _dot(qz_ref[h], k_ref[h * LANES:(h + 1) * LANES, :])
            for c in range(2 * bq // rc):
                rows = slice(c * rc, (c + 1) * rc)
                sc = s_ref[rows, :]
                if masked:
                    sc = jnp.where(rel <= base + (c * rc) % bq, sc, NEG_BIG)
                tile_max = sc[:, :LANES]
                for j in range(1, n_tiles):
                    tile_max = jnp.maximum(tile_max, sc[:, j * LANES:(j + 1) * LANES])
                m_prev = m_ref[h, rows, :]
                m_next = jnp.maximum(m_prev, jnp.max(tile_max, axis=1, keepdims=True))
                p = jnp.exp2(sc - jnp.concatenate([m_next] * n_tiles, axis=1))
                alpha = jnp.exp2(m_prev - m_next)
                tile_sum = p[:, :LANES]
                for j in range(1, n_tiles):
                    tile_sum = tile_sum + p[:, j * LANES:(j + 1) * LANES]
                l_ref[h, rows, :] = alpha * l_ref[h, rows, :] + tile_sum
                m_ref[h, rows, :] = m_next
                a_ref[rows, :] = alpha
                p_ref[rows, :] = p.astype(BF16)
            acc_ref[h] = a_ref[...] * acc_ref[h] + _dot(p_ref[...], v_ref[:, h * LANES:(h + 1) * LANES])

    on_diag = ki * bk + bk - 1 > qi * bq

    @pl.when(on_diag)
    def _():
        body(True)

    @pl.when(jnp.logical_not(on_diag))
    def _():
        body(False)

    @pl.when(ki == last_ki)
    def _():
        lam = lam_ref[0]
        for h in range(N_HEADS):
            l1 = jnp.sum(l_ref[h, :bq, :], axis=1, keepdims=True)
            l2 = jnp.sum(l_ref[h, bq:, :], axis=1, keepdims=True)
            o = _subln(acc_ref[h, :bq, :], l1, acc_ref[h, bq:, :], l2, lam, g_ref[...], out_scale)
            o_ref[:, h * LANES:(h + 1) * LANES] = o.astype(BF16)


def _prompt_attn(q, kt, v, lam, subln_g, out_scale, *, bq, bk, rc):
    b, s, _ = q.shape
    assert s % bq == 0 and s % bk == 0 and bq % rc == 0
    pairs = [(i, j) for i in range(s // bq) for j in range((i * bq + bq - 1) // bk + 1)]
    qi_tab = jnp.asarray(np.array([p[0] for p in pairs], np.int32))
    ki_tab = jnp.asarray(np.array([p[1] for p in pairs], np.int32))
    q_spec = pl.BlockSpec((None, bq, QK_WIDTH), lambda bi, p, qt, kt: (bi, qt[p], 0))
    k_spec = pl.BlockSpec((None, QK_WIDTH, bk), lambda bi, p, qt, kt: (bi, 0, kt[p]))
    v_spec = pl.BlockSpec((None, bk, ATTN_WIDTH), lambda bi, p, qt, kt: (bi, kt[p], 0))
    grid_spec = pltpu.PrefetchScalarGridSpec(
        num_scalar_prefetch=2,
        grid=(b, len(pairs)),
        in_specs=[
            pl.BlockSpec(memory_space=pltpu.SMEM),
            q_spec, k_spec, v_spec,
            pl.BlockSpec((1, V_DIM), lambda bi, p, qt, kt: (0, 0)),
        ],
        out_specs=q_spec,
        scratch_shapes=[
            pltpu.VMEM((N_HEADS, 2 * bq, LANES), BF16),
            pltpu.VMEM((N_HEADS, 2 * bq, LANES), F32),
            pltpu.VMEM((N_HEADS, 2 * bq, LANES), F32),
            pltpu.VMEM((N_HEADS, 2 * bq, V_DIM), F32),
            pltpu.VMEM((2 * bq, bk), F32),
            pltpu.VMEM((2 * bq, bk), BF16),
            pltpu.VMEM((2 * bq, LANES), F32),
        ],
    )
    return pl.pallas_call(
        functools.partial(_prompt_attn_kernel, bq=bq, bk=bk, rc=rc, out_scale=out_scale),
        grid_spec=grid_spec,
        out_shape=jax.ShapeDtypeStruct((b, s, ATTN_WIDTH), BF16),
        compiler_params=pltpu.CompilerParams(dimension_semantics=("arbitrary", "arbitrary"),
                                             vmem_limit_bytes=VMEM_LIMIT),
        name="prompt_attn",
    )(qi_tab, ki_tab, lam, q, kt, v, subln_g)


def _decode_attn_kernel(pt_ref, lam_ref, q_ref, kn_ref, vn_ref, g_ref, e_ref, hm_ref, kc_ref, vc_ref,
                        o_ref, qbd_ref, m_ref, l_ref, acc_ref, kbuf_ref, vbuf_ref, sem_ref,
                        *, n_seq, n_grp, grp, t_new, out_scale):
    i = pl.program_id(0)
    g = pl.program_id(1)
    step = i * n_grp + g
    slot = step % 2
    n_rows = 2 * N_HEADS * t_new
    rows_per_head = 2 * t_new

    def page_copies(seq, group, buf_slot):
        copies = []
        for j in range(grp):
            page = pt_ref[seq, group * grp + j]
            copies.append(pltpu.make_async_copy(kc_ref.at[page], kbuf_ref.at[buf_slot, j], sem_ref.at[0, buf_slot]))
            copies.append(pltpu.make_async_copy(vc_ref.at[page], vbuf_ref.at[buf_slot, j], sem_ref.at[1, buf_slot]))
        return copies

    @pl.when(step == 0)
    def _():
        for cp in page_copies(0, 0, 0):
            cp.start()

    @pl.when(step + 1 < n_seq * n_grp)
    def _():
        wrap = g + 1 == n_grp
        for cp in page_copies(jnp.where(wrap, i + 1, i), jnp.where(wrap, 0, g + 1), 1 - slot):
            cp.start()

    for cp in page_copies(i, g, slot):
        cp.wait()

    @pl.when(g == 0)
    def _():
        qf = q_ref[...].astype(F32)
        lane = lax.broadcasted_iota(jnp.int32, (t_new, QK_WIDTH), 1)
        parts = [jnp.where((lane >= r * HEAD_DIM) & (lane < (r + 1) * HEAD_DIM), qf, 0.0)
                 for r in range(2 * N_HEADS)]
        qbd_ref[...] = jnp.concatenate(parts, axis=0).astype(BF16)
        m_ref[...] = jnp.full(m_ref.shape, NEG_BIG, F32)
        l_ref[...] = jnp.zeros(l_ref.shape, F32)
        acc_ref[...] = jnp.zeros(acc_ref.shape, F32)

    def softmax_update(s):
        p, alpha, m_next, l_next = _online_softmax_step(s, m_ref[...], l_ref[...])
        m_ref[...] = m_next
        l_ref[...] = l_next
        return p.astype(BF16), alpha

    kt = jnp.concatenate([kbuf_ref[slot, j].astype(BF16) for j in range(grp)], axis=1)
    pb, alpha = softmax_update(_dot(qbd_ref[...], kt))
    p_rows = jnp.concatenate([pb[:, j * PAGE_SIZE:(j + 1) * PAGE_SIZE] for j in range(grp)], axis=0)
    p4 = _dot(p_rows, e_ref[...])
    hm = hm_ref[...]
    p4 = jnp.concatenate([(p4[j * n_rows:(j + 1) * n_rows, :] * hm).astype(BF16) for j in range(grp)], axis=1)
    vm = jnp.concatenate([vbuf_ref[slot, j].astype(BF16) for j in range(grp)], axis=0)
    acc_ref[...] = alpha * acc_ref[...] + _dot(p4, vm)

    @pl.when(g == n_grp - 1)
    def _():
        pad = jnp.zeros((LANES - t_new, QK_WIDTH), F32)
        kn = jnp.concatenate([kn_ref[...], pad], axis=0).astype(BF16)
        vn = jnp.concatenate([vn_ref[...], pad], axis=0).astype(BF16)
        row_t = lax.broadcasted_iota(jnp.int32, (n_rows, LANES), 0) % t_new
        col = lax.broadcasted_iota(jnp.int32, (n_rows, LANES), 1)
        s = jnp.where(col <= row_t, _dot_nt(qbd_ref[...], kn), NEG_BIG)
        pn, alpha_n = softmax_update(s)
        pv = _dot(pn, vn)
        pv = jnp.concatenate([pv[h * rows_per_head:(h + 1) * rows_per_head, h * LANES:(h + 1) * LANES]
                              for h in range(N_HEADS)], axis=0)
        acc = alpha_n * acc_ref[...] + pv
        lsum = l_ref[...]
        lam = lam_ref[0]
        for h in range(N_HEADS):
            r1 = slice(h * rows_per_head, h * rows_per_head + t_new)
            r2 = slice(h * rows_per_head + t_new, (h + 1) * rows_per_head)
            o = _subln(acc[r1], lsum[r1], acc[r2], lsum[r2], lam, g_ref[...], out_scale)
            o_ref[:, h * LANES:(h + 1) * LANES] = o.astype(BF16)


def _decode_attn(q, k_new, v_new, cache_kt, cache_vm, page_table, lam, subln_g, out_scale, *, grp):
    n, t_new, _ = q.shape
    n_pages = page_table.shape[1]
    assert n_pages % grp == 0
    n_grp = n_pages // grp
    n_rows = 2 * N_HEADS * t_new
    tok_spec = pl.BlockSpec((None, t_new, QK_WIDTH), lambda i, g, pt: (i, 0, 0))
    vrows = PAGE_SIZE * N_HEADS
    col = np.arange(vrows)
    expand = jnp.asarray(col[None, :] // N_HEADS == np.arange(PAGE_SIZE)[:, None], BF16)
    head_mask = jnp.asarray(col[None, :] % N_HEADS == (np.arange(n_rows) // (2 * t_new))[:, None], F32)

    grid_spec = pltpu.PrefetchScalarGridSpec(
        num_scalar_prefetch=1,
        grid=(n, n_grp),
        in_specs=[pl.BlockSpec(memory_space=pltpu.SMEM), tok_spec, tok_spec, tok_spec,
                  pl.BlockSpec((1, V_DIM), lambda i, g, pt: (0, 0)),
                  pl.BlockSpec((PAGE_SIZE, vrows), lambda i, g, pt: (0, 0)),
                  pl.BlockSpec((n_rows, vrows), lambda i, g, pt: (0, 0)),
                  pl.BlockSpec(memory_space=pl.ANY), pl.BlockSpec(memory_space=pl.ANY)],
        out_specs=tok_spec,
        scratch_shapes=[
            pltpu.VMEM((n_rows, QK_WIDTH), BF16),
            pltpu.VMEM((n_rows, LANES), F32),
            pltpu.VMEM((n_rows, LANES), F32),
            pltpu.VMEM((n_rows, V_DIM), F32),
            pltpu.VMEM((2, grp, QK_WIDTH, PAGE_SIZE), F32),
            pltpu.VMEM((2, grp, vrows, V_DIM), F32),
            pltpu.SemaphoreType.DMA((2, 2)),
        ],
    )
    return pl.pallas_call(
        functools.partial(_decode_attn_kernel, n_seq=n, n_grp=n_grp, grp=grp, t_new=t_new, out_scale=out_scale),
        grid_spec=grid_spec,
        out_shape=jax.ShapeDtypeStruct((n, t_new, ATTN_WIDTH), BF16),
        compiler_params=pltpu.CompilerParams(dimension_semantics=("arbitrary", "arbitrary"),
                                             vmem_limit_bytes=VMEM_LIMIT),
        name="decode_attn",
    )(page_table, lam, q, k_new, v_new, subln_g, expand, head_mask, cache_kt, cache_vm)


def _gelu_tanh(x):
    return 0.5 * x * (1.0 + jnp.tanh(math.sqrt(2.0 / math.pi) * (x + 0.044715 * (x * x * x))))


def _merge_ffn_kernel(x_ref, c_ref, o_ref, ctx_ref, wg_ref, wa_ref, wb_ref, wo_ref, g1_ref, b1_ref,
                      wup_ref, fw_ref, fb_ref, wdn_ref, g2_ref, b2_ref,
                      y_ref, nffn_ref, uext_ref, *, nb, tl, n_len, n_chunk, alpha):
    s = pl.program_id(1)
    lo = FFN_PAD - (FFN_TAPS - 1)
    m = nb * tl

    @pl.when(s == 0)
    def _():
        uext_ref[:, lo:FFN_PAD, :] = ctx_ref[...]

    x = x_ref[...]
    gates = _dot(x.astype(BF16), wg_ref[...])
    mixed = (jax.nn.sigmoid(gates[:, :D_MODEL]) * _dot(c_ref[...], wa_ref[...])
             + jax.nn.sigmoid(gates[:, D_MODEL:]) * _dot(o_ref[...], wb_ref[...]))
    h = _layer_norm(alpha * x + _dot(mixed.astype(BF16), wo_ref[...]), g1_ref[...], b1_ref[...])
    hb = h.astype(BF16)

    cw = D_FF // n_chunk
    f = jnp.zeros((m, D_MODEL), F32)
    for ci in range(n_chunk):
        cs = slice(ci * cw, (ci + 1) * cw)
        uf = _dot(hb, wup_ref[:, ci * cw:(ci + 1) * cw])
        vf = _dot(hb, wup_ref[:, D_FF + ci * cw:D_FF + (ci + 1) * cw])
        uf3 = uf.reshape(nb, tl, cw)
        uext_ref[:, FFN_PAD:FFN_PAD + tl, cs] = uf3
        cf = fb_ref[:, cs].reshape(1, 1, cw) + fw_ref[FFN_TAPS - 1:FFN_TAPS, cs].reshape(1, 1, cw) * uf3
        for j in range(FFN_TAPS - 1):
            cf = cf + fw_ref[j:j + 1, cs].reshape(1, 1, cw) * uext_ref[:, lo + j:lo + j + tl, cs]
        gl = _gelu_tanh(cf.reshape(m, cw)) * vf
        f = f + _dot(gl.astype(BF16), wdn_ref[ci * cw:(ci + 1) * cw, :])
    y_ref[...] = _layer_norm(alpha * h + f, g2_ref[...], b2_ref[...])

    nffn_ref[...] = uext_ref[:, lo + tl:FFN_PAD + tl, :]
    if n_len > 1:
        @pl.when(s + 1 < n_len)
        def _():
            uext_ref[:, lo:FFN_PAD, :] = uext_ref[:, lo + tl:FFN_PAD + tl, :]


def _merge_ffn(x2, c_act, o_n, ctx, w_gate, w_br_a, w_br_b, w_o, g1, b1, w_up, fw, fb, w_down, g2, b2,
               *, nb, tl, n_chunk, alpha):
    n = ctx.shape[0]
    length = x2.shape[0] // n
    assert n % nb == 0 and length % tl == 0 and D_FF % (n_chunk * LANES) == 0
    n_seq, n_len = n // nb, length // tl
    assert n_len == 1 or (nb == 1 and tl >= FFN_TAPS - 1)
    m = nb * tl
    row_map = lambda i, s: (i * n_len + s, 0)
    seq_spec = pl.BlockSpec((nb, FFN_TAPS - 1, D_FF), lambda i, s: (i, 0, 0))
    return pl.pallas_call(
        functools.partial(_merge_ffn_kernel, nb=nb, tl=tl, n_len=n_len, n_chunk=n_chunk, alpha=alpha),
        grid=(n_seq, n_len),
        in_specs=[
            pl.BlockSpec((m, D_MODEL), row_map),
            pl.BlockSpec((m, CONV_CH), row_map),
            pl.BlockSpec((m, ATTN_WIDTH), row_map),
            seq_spec,
            _const_spec((D_MODEL, GATE_WIDTH)),
            _const_spec((CONV_CH, D_MODEL)),
            _const_spec((ATTN_WIDTH, D_MODEL)),
            _const_spec((D_MODEL, D_MODEL)),
            _const_spec((1, D_MODEL)), _const_spec((1, D_MODEL)),
            _const_spec((D_MODEL, 2 * D_FF)),
            _const_spec((FFN_TAPS, D_FF)),
            _const_spec((1, D_FF)),
            _const_spec((D_FF, D_MODEL)),
            _const_spec((1, D_MODEL)), _const_spec((1, D_MODEL)),
        ],
        out_specs=[pl.BlockSpec((m, D_MODEL), row_map), seq_spec],
        out_shape=[jax.ShapeDtypeStruct((n * length, D_MODEL), F32),
                   jax.ShapeDtypeStruct((n, FFN_TAPS - 1, D_FF), F32)],
        scratch_shapes=[pltpu.VMEM((nb, FFN_PAD + tl, D_FF), F32)],
        compiler_params=pltpu.CompilerParams(dimension_semantics=("arbitrary", "arbitrary"),
                                             vmem_limit_bytes=VMEM_LIMIT),
        name="merge_ffn",
    )(x2, c_act, o_n, ctx, w_gate, w_br_a, w_br_b, w_o, g1, b1, w_up, fw, fb, w_down, g2, b2)


def _rope_tables(pos):
    half = ROT_DIM // 2
    inv = ROPE_THETA ** (-jnp.arange(half, dtype=F32) / half)
    dim = np.arange(LANES) % HEAD_DIM
    ang = pos.astype(F32)[:, None] * inv[dim % half][None, :]
    first = jnp.asarray(dim < half)[None, :]
    second = jnp.asarray((dim >= half) & (dim < ROT_DIM))[None, :]
    cos, sin = jnp.cos(ang), jnp.sin(ang)
    return (jnp.where(first | second, cos, 1.0), jnp.where(first, -sin, 0.0), jnp.where(second, sin, 0.0))


def _tile_len(length, cap):
    t = min(length, cap)
    while length % t:
        t //= 2
    return t


def kernel(x_prompt, x_sample, cache_k, cache_v, state_conv, state_ffn, page_table, w_in, lambda_q1, lambda_k1,
           lambda_q2, lambda_k2, conv_dw_w, conv_dw_b, conv_ln_g, conv_ln_b, w_br_a, subln_g, w_br_b, w_o,
           ln1_g, ln1_b, w_up, ffn_dw_w, ffn_dw_b, w_down, ln2_g, ln2_b):
    b, s, _ = x_prompt.shape
    db, t, _ = x_sample.shape
    depth = w_in.shape[0]
    n_pages = page_table.shape[1]
    past = n_pages * PAGE_SIZE
    alpha = (2 * depth) ** 0.25
    n_phys = cache_k.shape[1]

    tabs_p = _rope_tables(jnp.arange(s, dtype=jnp.int32))
    tabs_s = tuple(jnp.tile(tb, (db, 1)) for tb in _rope_tables(past + jnp.arange(t, dtype=jnp.int32)))
    conv_zero = jnp.zeros((b, CONV_TAPS - 1, CONV_CH), F32)
    ffn_zero = jnp.zeros((b, FFN_TAPS - 1, D_FF), F32)

    tl_p = _tile_len(s, 256)
    bq = _tile_len(s, 512)
    grp = math.gcd(n_pages, 16)
    nb_s = db
    while nb_s * t > 256 and nb_s % 2 == 0:
        nb_s //= 2

    yp = x_prompt.reshape(b * s, D_MODEL)
    ys = x_sample.reshape(db * t, D_MODEL)
    outs = [[] for _ in range(8)]
    row = lambda a: a.reshape(1, -1)
    for l in range(depth):
        lam_init = 0.8 - 0.6 * math.exp(-0.3 * l)
        lam = (jnp.exp(jnp.sum(lambda_q1[l] * lambda_k1[l])) - jnp.exp(jnp.sum(lambda_q2[l] * lambda_k2[l]))
               + lam_init).reshape(1).astype(F32)
        out_scale = 1.0 - lam_init
        w_qkvu = w_in[l][:, :QKVU_WIDTH].astype(BF16)
        w_gate = w_in[l][:, QKVU_WIDTH:].astype(BF16)
        conv_w = (conv_dw_w[l], row(conv_dw_b[l]), row(conv_ln_g[l]), row(conv_ln_b[l]))
        ffn_w = (w_gate, w_br_a[l].astype(BF16), w_br_b[l].astype(BF16), w_o[l].astype(BF16), row(ln1_g[l]),
                 row(ln1_b[l]), w_up[l].astype(BF16), ffn_dw_w[l], row(ffn_dw_b[l]), w_down[l].astype(BF16),
                 row(ln2_g[l]), row(ln2_b[l]))
        g_sub = row(subln_g[l])

        q, kt, vf, c_act, cp, ktb, vb = _in_proj(yp, conv_zero, w_qkvu, *tabs_p, *conv_w, nb=1, tl=tl_p,
                                                 k_transposed=True)
        o_n = _prompt_attn(q.reshape(b, s, QK_WIDTH), ktb, vb.reshape(b, s, ATTN_WIDTH),
                           lam, g_sub, out_scale, bq=bq, bk=bq, rc=min(bq, 32))
        yp, fp = _merge_ffn(yp, c_act, o_n.reshape(b * s, ATTN_WIDTH), ffn_zero, *ffn_w,
                            nb=1, tl=tl_p, n_chunk=1, alpha=alpha)
        outs[0].append(jnp.transpose(kt.reshape(b, N_HEADS, 2, HEAD_DIM, s), (0, 4, 1, 2, 3)))
        outs[1].append(vf.reshape(b, s, N_HEADS, V_DIM))
        outs[2].append(cp)
        outs[3].append(fp)

        q, kf, vf, c_act, cs = _in_proj(ys, state_conv[l], w_qkvu, *tabs_s, *conv_w, nb=nb_s, tl=t,
                                        k_transposed=False)
        cache_kt = jnp.transpose(cache_k[l], (0, 2, 3, 4, 1)).reshape(n_phys, QK_WIDTH, PAGE_SIZE)
        cache_vm = cache_v[l].reshape(n_phys, PAGE_SIZE * N_HEADS, V_DIM)
        o_n = _decode_attn(q.reshape(db, t, QK_WIDTH), kf.reshape(db, t, QK_WIDTH), vf.reshape(db, t, ATTN_WIDTH),
                           cache_kt, cache_vm, page_table, lam, g_sub, out_scale, grp=grp)
        ys, fs = _merge_ffn(ys, c_act, o_n.reshape(db * t, ATTN_WIDTH), state_ffn[l], *ffn_w,
                            nb=nb_s, tl=t, n_chunk=1, alpha=alpha)
        outs[4].append(kf.reshape(db, t, N_HEADS, 2, HEAD_DIM))
        outs[5].append(vf.reshape(db, t, N_HEADS, V_DIM))
        outs[6].append(cs)
        outs[7].append(fs)

    stacked = [jnp.stack(o) for o in outs]
    return (yp.reshape(b, s, D_MODEL), ys.reshape(db, t, D_MODEL), *stacked)
```

```python
import functools
import math

import jax
import jax.numpy as jnp
import numpy as np
from jax import lax
from jax.experimental import pallas as pl
from jax.experimental.pallas import tpu as pltpu

D_MODEL = 1024
CONV_CH = D_MODEL // 2
CONV_TAPS = 31
N_HEADS = 4
HEAD_DIM = 64
V_DIM = 2 * HEAD_DIM
QK_WIDTH = N_HEADS * 2 * HEAD_DIM
ATTN_WIDTH = N_HEADS * V_DIM
ROT_DIM = HEAD_DIM // 4
ROPE_THETA = 500000.0
D_FF = 2816
FFN_TAPS = 3
LN_EPS = 1e-5
PAGE_SIZE = 128
QKVU_WIDTH = 2 * CONV_CH + 2 * QK_WIDTH + ATTN_WIDTH
GATE_WIDTH = 2 * D_MODEL

LANES = 128
SUBLANES = 8
VMEM_LIMIT = 56 * 1024 * 1024
CONV_PAD = 32
FFN_PAD = 8
NEG_BIG = -1e30
LOG2_E = math.log2(math.e)

BF16 = jnp.bfloat16
F32 = jnp.float32


def _const_spec(shape):
    nd = len(shape)
    return pl.BlockSpec(shape, lambda *_: (0,) * nd, pipeline_mode=pl.Buffered(1))


def _layer_norm(x, g, b):
    mu = jnp.mean(x, axis=-1, keepdims=True)
    xc = x - mu
    var = jnp.mean(xc * xc, axis=-1, keepdims=True)
    return xc * lax.rsqrt(var + LN_EPS) * g + b


def _dot(a, b):
    return jnp.dot(a, b, preferred_element_type=F32)


def _dot_nt(a, b):
    return lax.dot_general(a, b, (((1,), (1,)), ((), ())), preferred_element_type=F32)


def _in_proj_kernel(x_ref, w_ref, ctx_ref, cos_ref, sa_ref, sb_ref, dww_ref, dwb_ref, lng_ref, lnb_ref,
                    q_ref, kf_ref, vf_ref, c_ref, nconv_ref, *rest, nb, tl, n_len, k_transposed):
    if k_transposed:
        kb_ref, vb_ref, uext_ref = rest
    else:
        (uext_ref,) = rest
    s = pl.program_id(1)
    lo = CONV_PAD - (CONV_TAPS - 1)

    @pl.when(s == 0)
    def _():
        uext_ref[:, :lo, :] = jnp.zeros((nb, lo, CONV_CH), F32)
        uext_ref[:, lo:CONV_PAD, :] = ctx_ref[...]

    xb = x_ref[...].astype(BF16)
    proj = _dot(xb, w_ref[:, 2 * CONV_CH:])
    cos = cos_ref[...]
    sa = sa_ref[...]
    sb = sb_ref[...]
    half = ROT_DIM // 2
    q0 = 0
    k0 = q0 + QK_WIDTH
    v0 = k0 + QK_WIDTH
    scale = HEAD_DIM ** -0.5 * LOG2_E
    for h in range(N_HEADS):
        c0, c1 = h * LANES, (h + 1) * LANES
        t = proj[:, q0 + c0:q0 + c1]
        r = t * cos + pltpu.roll(t, LANES - half, axis=1) * sa + pltpu.roll(t, half, axis=1) * sb
        q_ref[:, c0:c1] = (r * scale).astype(BF16)
        t = proj[:, k0 + c0:k0 + c1]
        r = t * cos + pltpu.roll(t, LANES - half, axis=1) * sa + pltpu.roll(t, half, axis=1) * sb
        if k_transposed:
            rt = r.T
            kf_ref[c0:c1, :] = rt
            kb_ref[c0:c1, :] = rt.astype(BF16)
        else:
            kf_ref[:, c0:c1] = r
    v = proj[:, v0:v0 + ATTN_WIDTH]
    if k_transposed:
        for h in range(N_HEADS):
            vf_ref[pl.ds(h, nb * tl, stride=N_HEADS), :] = v[:, h * LANES:(h + 1) * LANES]
        vb_ref[...] = v.astype(BF16)
    else:
        vf_ref[...] = v

    glu = _dot(xb, w_ref[:, :2 * CONV_CH])
    u = glu[:, :CONV_CH] * jax.nn.sigmoid(glu[:, CONV_CH:2 * CONV_CH])
    uext_ref[:, CONV_PAD:CONV_PAD + tl, :] = u.reshape(nb, tl, CONV_CH)

    acc = jnp.broadcast_to(dwb_ref[...].reshape(1, 1, CONV_CH), (nb, tl, CONV_CH))
    for b in range(SUBLANES):
        span = tl if b == 0 else tl + SUBLANES
        part = None
        for a in range((CONV_PAD + SUBLANES - 1) // SUBLANES + 1):
            j = SUBLANES * a + b - lo
            if 0 <= j < CONV_TAPS:
                term = (dww_ref[j:j + 1, :].reshape(1, 1, CONV_CH)
                        * uext_ref[:, SUBLANES * a:SUBLANES * a + span, :])
                part = term if part is None else part + term
        acc = acc + part[:, b:b + tl, :]
    c = _layer_norm(acc.reshape(nb * tl, CONV_CH), lng_ref[...], lnb_ref[...])
    c_ref[...] = (c * jax.nn.sigmoid(c)).astype(BF16)

    nconv_ref[...] = uext_ref[:, lo + tl:CONV_PAD + tl, :]
    if n_len > 1:
        @pl.when(s + 1 < n_len)
        def _():
            uext_ref[:, lo:CONV_PAD, :] = uext_ref[:, lo + tl:CONV_PAD + tl, :]


def _in_proj(x2, ctx, w_qkvu, cos, sa, sb, dww, dwb, lng, lnb, *, nb, tl, k_transposed):
    n = ctx.shape[0]
    length = x2.shape[0] // n
    assert n % nb == 0 and length % tl == 0
    n_seq, n_len = n // nb, length // tl
    assert n_len == 1 or (nb == 1 and tl >= CONV_TAPS - 1)
    assert nb == 1 or not k_transposed
    m = nb * tl
    row_map = lambda i, s: (i * n_len + s, 0)
    tab_spec = pl.BlockSpec((m, LANES), lambda i, s: (s, 0))
    seq_spec = pl.BlockSpec((nb, CONV_TAPS - 1, CONV_CH), lambda i, s: (i, 0, 0))
    out_rows = lambda width: pl.BlockSpec((m, width), row_map)
    rows = n * length
    if k_transposed:
        k_spec = pl.BlockSpec((None, QK_WIDTH, tl), lambda i, s: (i, 0, s))
        k_shape = (n, QK_WIDTH, length)
        v_spec = pl.BlockSpec((m * N_HEADS, V_DIM), row_map)
        v_shape = (rows * N_HEADS, V_DIM)
        extra_specs = [k_spec, out_rows(ATTN_WIDTH)]
        extra_shapes = [jax.ShapeDtypeStruct(k_shape, BF16),
                        jax.ShapeDtypeStruct((rows, ATTN_WIDTH), BF16)]
    else:
        k_spec = out_rows(QK_WIDTH)
        k_shape = (rows, QK_WIDTH)
        v_spec = out_rows(ATTN_WIDTH)
        v_shape = (rows, ATTN_WIDTH)
        extra_specs, extra_shapes = [], []
    return pl.pallas_call(
        functools.partial(_in_proj_kernel, nb=nb, tl=tl, n_len=n_len, k_transposed=k_transposed),
        grid=(n_seq, n_len),
        in_specs=[
            pl.BlockSpec((m, D_MODEL), row_map),
            _const_spec((D_MODEL, QKVU_WIDTH)),
            seq_spec,
            tab_spec, tab_spec, tab_spec,
            _const_spec((CONV_TAPS, CONV_CH)),
            _const_spec((1, CONV_CH)), _const_spec((1, CONV_CH)), _const_spec((1, CONV_CH)),
        ],
        out_specs=[out_rows(QK_WIDTH), k_spec, v_spec, out_rows(CONV_CH), seq_spec] + extra_specs,
        out_shape=[
            jax.ShapeDtypeStruct((rows, QK_WIDTH), BF16),
            jax.ShapeDtypeStruct(k_shape, F32),
            jax.ShapeDtypeStruct(v_shape, F32),
            jax.ShapeDtypeStruct((rows, CONV_CH), BF16),
            jax.ShapeDtypeStruct((n, CONV_TAPS - 1, CONV_CH), F32),
        ] + extra_shapes,
        scratch_shapes=[pltpu.VMEM((nb, CONV_PAD + tl, CONV_CH), F32)],
        compiler_params=pltpu.CompilerParams(dimension_semantics=("arbitrary", "arbitrary"),
                                             vmem_limit_bytes=VMEM_LIMIT),
        name="in_proj",
    )(x2, w_qkvu, ctx, cos, sa, sb, dww, dwb, lng, lnb)


def _subln(o1, l1, o2, l2, lam, g, out_scale):
    o = o1 / l1 - lam * (o2 / l2)
    ms = jnp.mean(o * o, axis=-1, keepdims=True)
    return o * lax.rsqrt(ms + LN_EPS) * g * out_scale


def _online_softmax_step(s, m_prev, l_prev):
    m_next = jnp.maximum(m_prev, jnp.max(s, axis=1, keepdims=True))
    p = jnp.exp2(s - m_next[:, :1])
    alpha = jnp.exp2(m_prev - m_next)
    l_next = alpha * l_prev + jnp.sum(p, axis=1, keepdims=True)
    return p, alpha, m_next, l_next


def _prompt_attn_kernel(qi_ref, ki_ref, lam_ref, q_ref, k_ref, v_ref, g_ref, o_ref,
                        qz_ref, m_ref, l_ref, acc_ref, s_ref, p_ref, a_ref, *, bq, bk, rc, out_scale):
    p_idx = pl.program_id(1)
    qi = qi_ref[p_idx]
    ki = ki_ref[p_idx]
    last_ki = (qi * bq + bq - 1) // bk
    n_tiles = bk // LANES

    @pl.when(ki == 0)
    def _():
        lane = lax.broadcasted_iota(jnp.int32, (bq, LANES), 1)
        for h in range(N_HEADS):
            qh = q_ref[:, h * LANES:(h + 1) * LANES]
            qz_ref[h, :bq, :] = jnp.where(lane < HEAD_DIM, qh, jnp.zeros_like(qh))
            qz_ref[h, bq:, :] = jnp.where(lane >= HEAD_DIM, qh, jnp.zeros_like(qh))
        m_ref[...] = jnp.full(m_ref.shape, NEG_BIG, F32)
        l_ref[...] = jnp.zeros(l_ref.shape, F32)
        acc_ref[...] = jnp.zeros(acc_ref.shape, F32)

    def body(masked):
        if masked:
            rel = lax.broadcasted_iota(jnp.int32, (rc, bk), 1) - lax.broadcasted_iota(jnp.int32, (rc, bk), 0)
            base = qi * bq - ki * bk
        for h in range(N_HEADS):
            s_ref[...] = _dot(qz_ref[h], k_ref[h * LANES:(h + 1) * LANES, :])
            for c in range(2 * bq // rc):
                rows = slice(c * rc, (c + 1) * rc)
                sc = s_ref[rows, :]
                if masked:
                    sc = jnp.where(rel <= base + (c * rc) % bq, sc, NEG_BIG)
                tile_max = sc[:, :LANES]
                for j in range(1, n_tiles):
                    tile_max = jnp.maximum(tile_max, sc[:, j * LANES:(j + 1) * LANES])
                m_prev = m_ref[h, rows, :]
                m_next = jnp.maximum(m_prev, jnp.max(tile_max, axis=1, keepdims=True))
                p = jnp.exp2(sc - jnp.concatenate([m_next] * n_tiles, axis=1))
                alpha = jnp.exp2(m_prev - m_next)
                tile_sum = p[:, :LANES]
                for j in range(1, n_tiles):
                    tile_sum = tile_sum + p[:, j * LANES:(j + 1) * LANES]
                l_ref[h, rows, :] = alpha * l_ref[h, rows, :] + tile_sum
                m_ref[h, rows, :] = m_next
                a_ref[rows, :] = alpha
                p_ref[rows, :] = p.astype(BF16)
            acc_ref[h] = a_ref[...] * acc_ref[h] + _dot(p_ref[...], v_ref[:, h * LANES:(h + 1) * LANES])

    on_diag = ki * bk + bk - 1 > qi * bq

    @pl.when(on_diag)
    def _():
        body(True)

    @pl.when(jnp.logical_not(on_diag))
    def _():
        body(False)

    @pl.when(ki == last_ki)
    def _():
        lam = lam_ref[0]
        for h in range(N_HEADS):
            l1 = jnp.sum(l_ref[h, :bq, :], axis=1, keepdims=True)
            l2 = jnp.sum(l_ref[h, bq:, :], axis=1, keepdims=True)
            o = _subln(acc_ref[h, :bq, :], l1, acc_ref[h, bq:, :], l2, lam, g_ref[...], out_scale)
            o_ref[:, h * LANES:(h + 1) * LANES] = o.astype(BF16)


def _prompt_attn(q, kt, v, lam, subln_g, out_scale, *, bq, bk, rc):
    b, s, _ = q.shape
    assert s % bq == 0 and s % bk == 0 and bq % rc == 0
    pairs = [(i, j) for i in range(s // bq) for j in range((i * bq + bq - 1) // bk + 1)]
    qi_tab = jnp.asarray(np.array([p[0] for p in pairs], np.int32))
    ki_tab = jnp.asarray(np.array([p[1] for p in pairs], np.int32))
    q_spec = pl.BlockSpec((None, bq, QK_WIDTH), lambda bi, p, qt, kt: (bi, qt[p], 0))
    k_spec = pl.BlockSpec((None, QK_WIDTH, bk), lambda bi, p, qt, kt: (bi, 0, kt[p]))
    v_spec = pl.BlockSpec((None, bk, ATTN_WIDTH), lambda bi, p, qt, kt: (bi, kt[p], 0))
    grid_spec = pltpu.PrefetchScalarGridSpec(
        num_scalar_prefetch=2,
        grid=(b, len(pairs)),
        in_specs=[
            pl.BlockSpec(memory_space=pltpu.SMEM),
            q_spec, k_spec, v_spec,
            pl.BlockSpec((1, V_DIM), lambda bi, p, qt, kt: (0, 0)),
        ],
        out_specs=q_spec,
        scratch_shapes=[
            pltpu.VMEM((N_HEADS, 2 * bq, LANES), BF16),
            pltpu.VMEM((N_HEADS, 2 * bq, LANES), F32),
            pltpu.VMEM((N_HEADS, 2 * bq, LANES), F32),
            pltpu.VMEM((N_HEADS, 2 * bq, V_DIM), F32),
            pltpu.VMEM((2 * bq, bk), F32),
            pltpu.VMEM((2 * bq, bk), BF16),
            pltpu.VMEM((2 * bq, LANES), F32),
        ],
    )
    return pl.pallas_call(
        functools.partial(_prompt_attn_kernel, bq=bq, bk=bk, rc=rc, out_scale=out_scale),
        grid_spec=grid_spec,
        out_shape=jax.ShapeDtypeStruct((b, s, ATTN_WIDTH), BF16),
        compiler_params=pltpu.CompilerParams(dimension_semantics=("arbitrary", "arbitrary"),
                                             vmem_limit_bytes=VMEM_LIMIT),
        name="prompt_attn",
    )(qi_tab, ki_tab, lam, q, kt, v, subln_g)


def _decode_attn_kernel(pt_ref, lam_ref, q_ref, kn_ref, vn_ref, g_ref, e_ref, hm_ref, kc_ref, vc_ref,
                        o_ref, qbd_ref, m_ref, l_ref, acc_ref, kbuf_ref, vbuf_ref, sem_ref,
                        *, n_seq, n_grp, grp, t_new, out_scale):
    i = pl.program_id(0)
    g = pl.program_id(1)
    step = i * n_grp + g
    slot = step % 2
    n_rows = 2 * N_HEADS * t_new
    rows_per_head = 2 * t_new

    def page_copies(seq, group, buf_slot):
        copies = []
        for j in range(grp):
            page = pt_ref[seq, group * grp + j]
            copies.append(pltpu.make_async_copy(kc_ref.at[page], kbuf_ref.at[buf_slot, j], sem_ref.at[0, buf_slot]))
            copies.append(pltpu.make_async_copy(vc_ref.at[page], vbuf_ref.at[buf_slot, j], sem_ref.at[1, buf_slot]))
        return copies

    @pl.when(step == 0)
    def _():
        for cp in page_copies(0, 0, 0):
            cp.start()

    @pl.when(step + 1 < n_seq * n_grp)
    def _():
        wrap = g + 1 == n_grp
        for cp in page_copies(jnp.where(wrap, i + 1, i), jnp.where(wrap, 0, g + 1), 1 - slot):
            cp.start()

    for cp in page_copies(i, g, slot):
        cp.wait()

    @pl.when(g == 0)
    def _():
        qf = q_ref[...].astype(F32)
        lane = lax.broadcasted_iota(jnp.int32, (t_new, QK_WIDTH), 1)
        parts = [jnp.where((lane >= r * HEAD_DIM) & (lane < (r + 1) * HEAD_DIM), qf, 0.0)
                 for r in range(2 * N_HEADS)]
        qbd_ref[...] = jnp.concatenate(parts, axis=0).astype(BF16)
        m_ref[...] = jnp.full(m_ref.shape, NEG_BIG, F32)
        l_ref[...] = jnp.zeros(l_ref.shape, F32)
        acc_ref[...] = jnp.zeros(acc_ref.shape, F32)

    def softmax_update(s):
        p, alpha, m_next, l_next = _online_softmax_step(s, m_ref[...], l_ref[...])
        m_ref[...] = m_next
        l_ref[...] = l_next
        return p.astype(BF16), alpha

    kt = jnp.concatenate([kbuf_ref[slot, j].astype(BF16) for j in range(grp)], axis=1)
    pb, alpha = softmax_update(_dot(qbd_ref[...], kt))
    p_rows = jnp.concatenate([pb[:, j * PAGE_SIZE:(j + 1) * PAGE_SIZE] for j in range(grp)], axis=0)
    p4 = _dot(p_rows, e_ref[...])
    hm = hm_ref[...]
    p4 = jnp.concatenate([(p4[j * n_rows:(j + 1) * n_rows, :] * hm).astype(BF16) for j in range(grp)], axis=1)
    vm = jnp.concatenate([vbuf_ref[slot, j].astype(BF16) for j in range(grp)], axis=0)
    acc_ref[...] = alpha * acc_ref[...] + _dot(p4, vm)

    @pl.when(g == n_grp - 1)
    def _():
        pad = jnp.zeros((LANES - t_new, QK_WIDTH), F32)
        kn = jnp.concatenate([kn_ref[...], pad], axis=0).astype(BF16)
        vn = jnp.concatenate([vn_ref[...], pad], axis=0).astype(BF16)
        row_t = lax.broadcasted_iota(jnp.int32, (n_rows, LANES), 0) % t_new
        col = lax.broadcasted_iota(jnp.int32, (n_rows, LANES), 1)
        s = jnp.where(col <= row_t, _dot_nt(qbd_ref[...], kn), NEG_BIG)
        pn, alpha_n = softmax_update(s)
        pv = _dot(pn, vn)
        pv = jnp.concatenate([pv[h * rows_per_head:(h + 1) * rows_per_head, h * LANES:(h + 1) * LANES]
                              for h in range(N_HEADS)], axis=0)
        acc = alpha_n * acc_ref[...] + pv
        lsum = l_ref[...]
        lam = lam_ref[0]
        for h in range(N_HEADS):
            r1 = slice(h * rows_per_head, h * rows_per_head + t_new)
            r2 = slice(h * rows_per_head + t_new, (h + 1) * rows_per_head)
            o = _subln(acc[r1], lsum[r1], acc[r2], lsum[r2], lam, g_ref[...], out_scale)
            o_ref[:, h * LANES:(h + 1) * LANES] = o.astype(BF16)


def _decode_attn(q, k_new, v_new, cache_kt, cache_vm, page_table, lam, subln_g, out_scale, *, grp):
    n, t_new, _ = q.shape
    n_pages = page_table.shape[1]
    assert n_pages % grp == 0
    n_grp = n_pages // grp
    n_rows = 2 * N_HEADS * t_new
    tok_spec = pl.BlockSpec((None, t_new, QK_WIDTH), lambda i, g, pt: (i, 0, 0))
    vrows = PAGE_SIZE * N_HEADS
    col = np.arange(vrows)
    expand = jnp.asarray(col[None, :] // N_HEADS == np.arange(PAGE_SIZE)[:, None], BF16)
    head_mask = jnp.asarray(col[None, :] % N_HEADS == (np.arange(n_rows) // (2 * t_new))[:, None], F32)

    grid_spec = pltpu.PrefetchScalarGridSpec(
        num_scalar_prefetch=1,
        grid=(n, n_grp),
        in_specs=[pl.BlockSpec(memory_space=pltpu.SMEM), tok_spec, tok_spec, tok_spec,
                  pl.BlockSpec((1, V_DIM), lambda i, g, pt: (0, 0)),
                  pl.BlockSpec((PAGE_SIZE, vrows), lambda i, g, pt: (0, 0)),
                  pl.BlockSpec((n_rows, vrows), lambda i, g, pt: (0, 0)),
                  pl.BlockSpec(memory_space=pl.ANY), pl.BlockSpec(memory_space=pl.ANY)],
        out_specs=tok_spec,
        scratch_shapes=[
            pltpu.VMEM((n_rows, QK_WIDTH), BF16),
            pltpu.VMEM((n_rows, LANES), F32),
            pltpu.VMEM((n_rows, LANES), F32),
            pltpu.VMEM((n_rows, V_DIM), F32),
            pltpu.VMEM((2, grp, QK_WIDTH, PAGE_SIZE), F32),
            pltpu.VMEM((2, grp, vrows, V_DIM), F32),
            pltpu.SemaphoreType.DMA((2, 2)),
        ],
    )
    return pl.pallas_call(
        functools.partial(_decode_attn_kernel, n_seq=n, n_grp=n_grp, grp=grp, t_new=t_new, out_scale=out_scale),
        grid_spec=grid_spec,
        out_shape=jax.ShapeDtypeStruct((n, t_new, ATTN_WIDTH), BF16),
        compiler_params=pltpu.CompilerParams(dimension_semantics=("arbitrary", "arbitrary"),
                                             vmem_limit_bytes=VMEM_LIMIT),
        name="decode_attn",
    )(page_table, lam, q, k_new, v_new, subln_g, expand, head_mask, cache_kt, cache_vm)


def _gelu_tanh(x):
    return 0.5 * x * (1.0 + jnp.tanh(math.sqrt(2.0 / math.pi) * (x + 0.044715 * (x * x * x))))


def _merge_ffn_kernel(x_ref, c_ref, o_ref, ctx_ref, wg_ref, wa_ref, wb_ref, wo_ref, g1_ref, b1_ref,
                      wup_ref, fw_ref, fb_ref, wdn_ref, g2_ref, b2_ref,
                      y_ref, nffn_ref, uext_ref, *, nb, tl, n_len, n_chunk, alpha):
    s = pl.program_id(1)
    lo = FFN_PAD - (FFN_TAPS - 1)
    m = nb * tl

    @pl.when(s == 0)
    def _():
        uext_ref[:, lo:FFN_PAD, :] = ctx_ref[...]

    x = x_ref[...]
    gates = _dot(x.astype(BF16), wg_ref[...])
    mixed = (jax.nn.sigmoid(gates[:, :D_MODEL]) * _dot(c_ref[...], wa_ref[...])
             + jax.nn.sigmoid(gates[:, D_MODEL:]) * _dot(o_ref[...], wb_ref[...]))
    h = _layer_norm(alpha * x + _dot(mixed.astype(BF16), wo_ref[...]), g1_ref[...], b1_ref[...])
    hb = h.astype(BF16)

    cw = D_FF // n_chunk
    f = jnp.zeros((m, D_MODEL), F32)
    for ci in range(n_chunk):
        cs = slice(ci * cw, (ci + 1) * cw)
        uf = _dot(hb, wup_ref[:, ci * cw:(ci + 1) * cw])
        vf = _dot(hb, wup_ref[:, D_FF + ci * cw:D_FF + (ci + 1) * cw])
        uf3 = uf.reshape(nb, tl, cw)
        uext_ref[:, FFN_PAD:FFN_PAD + tl, cs] = uf3
        cf = fb_ref[:, cs].reshape(1, 1, cw) + fw_ref[FFN_TAPS - 1:FFN_TAPS, cs].reshape(1, 1, cw) * uf3
        for j in range(FFN_TAPS - 1):
            cf = cf + fw_ref[j:j + 1, cs].reshape(1, 1, cw) * uext_ref[:, lo + j:lo + j + tl, cs]
        gl = _gelu_tanh(cf.reshape(m, cw)) * vf
        f = f + _dot(gl.astype(BF16), wdn_ref[ci * cw:(ci + 1) * cw, :])
    y_ref[...] = _layer_norm(alpha * h + f, g2_ref[...], b2_ref[...])

    nffn_ref[...] = uext_ref[:, lo + tl:FFN_PAD + tl, :]
    if n_len > 1:
        @pl.when(s + 1 < n_len)
        def _():
            uext_ref[:, lo:FFN_PAD, :] = uext_ref[:, lo + tl:FFN_PAD + tl, :]


def _merge_ffn(x2, c_act, o_n, ctx, w_gate, w_br_a, w_br_b, w_o, g1, b1, w_up, fw, fb, w_down, g2, b2,
               *, nb, tl, n_chunk, alpha):
    n = ctx.shape[0]
    length = x2.shape[0] // n
    assert n % nb == 0 and length % tl == 0 and D_FF % (n_chunk * LANES) == 0
    n_seq, n_len = n // nb, length // tl
    assert n_len == 1 or (nb == 1 and tl >= FFN_TAPS - 1)
    m = nb * tl
    row_map = lambda i, s: (i * n_len + s, 0)
    seq_spec = pl.BlockSpec((nb, FFN_TAPS - 1, D_FF), lambda i, s: (i, 0, 0))
    return pl.pallas_call(
        functools.partial(_merge_ffn_kernel, nb=nb, tl=tl, n_len=n_len, n_chunk=n_chunk, alpha=alpha),
        grid=(n_seq, n_len),
        in_specs=[
            pl.BlockSpec((m, D_MODEL), row_map),
            pl.BlockSpec((m, CONV_CH), row_map),
            pl.BlockSpec((m, ATTN_WIDTH), row_map),
            seq_spec,
            _const_spec((D_MODEL, GATE_WIDTH)),
            _const_spec((CONV_CH, D_MODEL)),
            _const_spec((ATTN_WIDTH, D_MODEL)),
            _const_spec((D_MODEL, D_MODEL)),
            _const_spec((1, D_MODEL)), _const_spec((1, D_MODEL)),
            _const_spec((D_MODEL, 2 * D_FF)),
            _const_spec((FFN_TAPS, D_FF)),
            _const_spec((1, D_FF)),
            _const_spec((D_FF, D_MODEL)),
            _const_spec((1, D_MODEL)), _const_spec((1, D_MODEL)),
        ],
        out_specs=[pl.BlockSpec((m, D_MODEL), row_map), seq_spec],
        out_shape=[jax.ShapeDtypeStruct((n * length, D_MODEL), F32),
                   jax.ShapeDtypeStruct((n, FFN_TAPS - 1, D_FF), F32)],
        scratch_shapes=[pltpu.VMEM((nb, FFN_PAD + tl, D_FF), F32)],
        compiler_params=pltpu.CompilerParams(dimension_semantics=("arbitrary", "arbitrary"),
                                             vmem_limit_bytes=VMEM_LIMIT),
        name="merge_ffn",
    )(x2, c_act, o_n, ctx, w_gate, w_br_a, w_br_b, w_o, g1, b1, w_up, fw, fb, w_down, g2, b2)


def _rope_tables(pos):
    half = ROT_DIM // 2
    inv = ROPE_THETA ** (-jnp.arange(half, dtype=F32) / half)
    dim = np.arange(LANES) % HEAD_DIM
    ang = pos.astype(F32)[:, None] * inv[dim % half][None, :]
    first = jnp.asarray(dim < half)[None, :]
    second = jnp.asarray((dim >= half) & (dim < ROT_DIM))[None, :]
    cos, sin = jnp.cos(ang), jnp.sin(ang)
    return (jnp.where(first | second, cos, 1.0), jnp.where(first, -sin, 0.0), jnp.where(second, sin, 0.0))


def _tile_len(length, cap):
    t = min(length, cap)
    while length % t:
        t //= 2
    return t


def kernel(x_prompt, x_sample, cache_k, cache_v, state_conv, state_ffn, page_table, w_in, lambda_q1, lambda_k1,
           lambda_q2, lambda_k2, conv_dw_w, conv_dw_b, conv_ln_g, conv_ln_b, w_br_a, subln_g, w_br_b, w_o,
           ln1_g, ln1_b, w_up, ffn_dw_w, ffn_dw_b, w_down, ln2_g, ln2_b):
    b, s, _ = x_prompt.shape
    db, t, _ = x_sample.shape
    depth = w_in.shape[0]
    n_pages = page_table.shape[1]
    past = n_pages * PAGE_SIZE
    alpha = (2 * depth) ** 0.25
    n_phys = cache_k.shape[1]

    tabs_p = _rope_tables(jnp.arange(s, dtype=jnp.int32))
    tabs_s = tuple(jnp.tile(tb, (db, 1)) for tb in _rope_tables(past + jnp.arange(t, dtype=jnp.int32)))
    conv_zero = jnp.zeros((b, CONV_TAPS - 1, CONV_CH), F32)
    ffn_zero = jnp.zeros((b, FFN_TAPS - 1, D_FF), F32)

    tl_p = _tile_len(s, 512)
    bq = _tile_len(s, 512)
    grp = math.gcd(n_pages, 32)
    nb_s = db
    while nb_s * t > 256 and nb_s % 2 == 0:
        nb_s //= 2

    yp = x_prompt.reshape(b * s, D_MODEL)
    ys = x_sample.reshape(db * t, D_MODEL)
    outs = [[] for _ in range(8)]
    row = lambda a: a.reshape(1, -1)
    for l in range(depth):
        lam_init = 0.8 - 0.6 * math.exp(-0.3 * l)
        lam = (jnp.exp(jnp.sum(lambda_q1[l] * lambda_k1[l])) - jnp.exp(jnp.sum(lambda_q2[l] * lambda_k2[l]))
               + lam_init).reshape(1).astype(F32)
        out_scale = 1.0 - lam_init
        w_qkvu = w_in[l][:, :QKVU_WIDTH].astype(BF16)
        w_gate = w_in[l][:, QKVU_WIDTH:].astype(BF16)
        conv_w = (conv_dw_w[l], row(conv_dw_b[l]), row(conv_ln_g[l]), row(conv_ln_b[l]))
        ffn_w = (w_gate, w_br_a[l].astype(BF16), w_br_b[l].astype(BF16), w_o[l].astype(BF16), row(ln1_g[l]),
                 row(ln1_b[l]), w_up[l].astype(BF16), ffn_dw_w[l], row(ffn_dw_b[l]), w_down[l].astype(BF16),
                 row(ln2_g[l]), row(ln2_b[l]))
        g_sub = row(subln_g[l])

        q, kt, vf, c_act, cp, ktb, vb = _in_proj(yp, conv_zero, w_qkvu, *tabs_p, *conv_w, nb=1, tl=tl_p,
                                                 k_transposed=True)
        o_n = _prompt_attn(q.reshape(b, s, QK_WIDTH), ktb, vb.reshape(b, s, ATTN_WIDTH),
                           lam, g_sub, out_scale, bq=bq, bk=bq, rc=min(bq, 32))
        yp, fp = _merge_ffn(yp, c_act, o_n.reshape(b * s, ATTN_WIDTH), ffn_zero, *ffn_w,
                            nb=1, tl=tl_p, n_chunk=1, alpha=alpha)
        outs[0].append(jnp.transpose(kt.reshape(b, N_HEADS, 2, HEAD_DIM, s), (0, 4, 1, 2, 3)))
        outs[1].append(vf.reshape(b, s, N_HEADS, V_DIM))
        outs[2].append(cp)
        outs[3].append(fp)

        q, kf, vf, c_act, cs = _in_proj(ys, state_conv[l], w_qkvu, *tabs_s, *conv_w, nb=nb_s, tl=t,
                                        k_transposed=False)
        cache_kt = jnp.transpose(cache_k[l], (0, 2, 3, 4, 1)).reshape(n_phys, QK_WIDTH, PAGE_SIZE)
        cache_vm = cache_v[l].reshape(n_phys, PAGE_SIZE * N_HEADS, V_DIM)
        o_n = _decode_attn(q.reshape(db, t, QK_WIDTH), kf.reshape(db, t, QK_WIDTH), vf.reshape(db, t, ATTN_WIDTH),
                           cache_kt, cache_vm, page_table, lam, g_sub, out_scale, grp=grp)
        ys, fs = _merge_ffn(ys, c_act, o_n.reshape(db * t, ATTN_WIDTH), state_ffn[l], *ffn_w,
                            nb=nb_s, tl=t, n_chunk=1, alpha=alpha)
        outs[4].append(kf.reshape(db, t, N_HEADS, 2, HEAD_DIM))
        outs[5].append(vf.reshape(db, t, N_HEADS, V_DIM))
        outs[6].append(cs)
        outs[7].append(fs)

    stacked = [jnp.stack(o) for o in outs]
    return (yp.reshape(b, s, D_MODEL), ys.reshape(db, t, D_MODEL), *stacked)
```

```python
import functools
import math

import jax
import jax.numpy as jnp
import numpy as np
from jax import lax
from jax.experimental import pallas as pl
from jax.experimental.pallas import tpu as pltpu

D_MODEL = 1024
CONV_CH = D_MODEL // 2
CONV_TAPS = 31
N_HEADS = 4
HEAD_DIM = 64
V_DIM = 2 * HEAD_DIM
QK_WIDTH = N_HEADS * 2 * HEAD_DIM
ATTN_WIDTH = N_HEADS * V_DIM
ROT_DIM = HEAD_DIM // 4
ROPE_THETA = 500000.0
D_FF = 2816
FFN_TAPS = 3
LN_EPS = 1e-5
PAGE_SIZE = 128
QKVU_WIDTH = 2 * CONV_CH + 2 * QK_WIDTH + ATTN_WIDTH
GATE_WIDTH = 2 * D_MODEL

LANES = 128
SUBLANES = 8
VMEM_LIMIT = 56 * 1024 * 1024
CONV_PAD = 32
FFN_PAD = 8
NEG_BIG = -1e30
LOG2_E = math.log2(math.e)

BF16 = jnp.bfloat16
F32 = jnp.float32


def _const_spec(shape):
    nd = len(shape)
    return pl.BlockSpec(shape, lambda *_: (0,) * nd, pipeline_mode=pl.Buffered(1))


def _layer_norm(x, g, b):
    mu = jnp.mean(x, axis=-1, keepdims=True)
    xc = x - mu
    var = jnp.mean(xc * xc, axis=-1, keepdims=True)
    return xc * lax.rsqrt(var + LN_EPS) * g + b


def _dot(a, b):
    return jnp.dot(a, b, preferred_element_type=F32)


def _dot_nt(a, b):
    return lax.dot_general(a, b, (((1,), (1,)), ((), ())), preferred_element_type=F32)


def _in_proj_kernel(x_ref, w_ref, ctx_ref, cos_ref, sa_ref, sb_ref, dww_ref, dwb_ref, lng_ref, lnb_ref,
                    q_ref, kf_ref, vf_ref, c_ref, nconv_ref, *rest, nb, tl, n_len, k_transposed):
    if k_transposed:
        kb_ref, vb_ref, uext_ref = rest
    else:
        (uext_ref,) = rest
    s = pl.program_id(1)
    lo = CONV_PAD - (CONV_TAPS - 1)

    @pl.when(s == 0)
    def _():
        uext_ref[:, :lo, :] = jnp.zeros((nb, lo, CONV_CH), F32)
        uext_ref[:, lo:CONV_PAD, :] = ctx_ref[...]

    xb = x_ref[...].astype(BF16)
    proj = _dot(xb, w_ref[:, 2 * CONV_CH:])
    cos = cos_ref[...]
    sa = sa_ref[...]
    sb = sb_ref[...]
    half = ROT_DIM // 2
    q0 = 0
    k0 = q0 + QK_WIDTH
    v0 = k0 + QK_WIDTH
    scale = HEAD_DIM ** -0.5 * LOG2_E
    for h in range(N_HEADS):
        c0, c1 = h * LANES, (h + 1) * LANES
        t = proj[:, q0 + c0:q0 + c1]
        r = t * cos + pltpu.roll(t, LANES - half, axis=1) * sa + pltpu.roll(t, half, axis=1) * sb
        q_ref[:, c0:c1] = (r * scale).astype(BF16)
        t = proj[:, k0 + c0:k0 + c1]
        r = t * cos + pltpu.roll(t, LANES - half, axis=1) * sa + pltpu.roll(t, half, axis=1) * sb
        if k_transposed:
            rt = r.T
            kf_ref[c0:c1, :] = rt
            kb_ref[c0:c1, :] = rt.astype(BF16)
        else:
            kf_ref[:, c0:c1] = r
    v = proj[:, v0:v0 + ATTN_WIDTH]
    if k_transposed:
        for h in range(N_HEADS):
            vf_ref[pl.ds(h, nb * tl, stride=N_HEADS), :] = v[:, h * LANES:(h + 1) * LANES]
        vb_ref[...] = v.astype(BF16)
    else:
        vf_ref[...] = v

    glu = _dot(xb, w_ref[:, :2 * CONV_CH])
    u = glu[:, :CONV_CH] * jax.nn.sigmoid(glu[:, CONV_CH:2 * CONV_CH])
    uext_ref[:, CONV_PAD:CONV_PAD + tl, :] = u.reshape(nb, tl, CONV_CH)

    acc = jnp.broadcast_to(dwb_ref[...].reshape(1, 1, CONV_CH), (nb, tl, CONV_CH))
    for b in range(SUBLANES):
        span = tl if b == 0 else tl + SUBLANES
        part = None
        for a in range((CONV_PAD + SUBLANES - 1) // SUBLANES + 1):
            j = SUBLANES * a + b - lo
            if 0 <= j < CONV_TAPS:
                term = (dww_ref[j:j + 1, :].reshape(1, 1, CONV_CH)
                        * uext_ref[:, SUBLANES * a:SUBLANES * a + span, :])
                part = term if part is None else part + term
        acc = acc + part[:, b:b + tl, :]
    c = _layer_norm(acc.reshape(nb * tl, CONV_CH), lng_ref[...], lnb_ref[...])
    c_ref[...] = (c * jax.nn.sigmoid(c)).astype(BF16)

    nconv_ref[...] = uext_ref[:, lo + tl:CONV_PAD + tl, :]
    if n_len > 1:
        @pl.when(s + 1 < n_len)
        def _():
            uext_ref[:, lo:CONV_PAD, :] = uext_ref[:, lo + tl:CONV_PAD + tl, :]


def _in_proj(x2, ctx, w_qkvu, cos, sa, sb, dww, dwb, lng, lnb, *, nb, tl, k_transposed):
    n = ctx.shape[0]
    length = x2.shape[0] // n
    assert n % nb == 0 and length % tl == 0
    n_seq, n_len = n // nb, length // tl
    assert n_len == 1 or (nb == 1 and tl >= CONV_TAPS - 1)
    assert nb == 1 or not k_transposed
    m = nb * tl
    row_map = lambda i, s: (i * n_len + s, 0)
    tab_spec = pl.BlockSpec((m, LANES), lambda i, s: (s, 0))
    seq_spec = pl.BlockSpec((nb, CONV_TAPS - 1, CONV_CH), lambda i, s: (i, 0, 0))
    out_rows = lambda width: pl.BlockSpec((m, width), row_map)
    rows = n * length
    if k_transposed:
        k_spec = pl.BlockSpec((None, QK_WIDTH, tl), lambda i, s: (i, 0, s))
        k_shape = (n, QK_WIDTH, length)
        v_spec = pl.BlockSpec((m * N_HEADS, V_DIM), row_map)
        v_shape = (rows * N_HEADS, V_DIM)
        extra_specs = [k_spec, out_rows(ATTN_WIDTH)]
        extra_shapes = [jax.ShapeDtypeStruct(k_shape, BF16),
                        jax.ShapeDtypeStruct((rows, ATTN_WIDTH), BF16)]
    else:
        k_spec = out_rows(QK_WIDTH)
        k_shape = (rows, QK_WIDTH)
        v_spec = out_rows(ATTN_WIDTH)
        v_shape = (rows, ATTN_WIDTH)
        extra_specs, extra_shapes = [], []
    return pl.pallas_call(
        functools.partial(_in_proj_kernel, nb=nb, tl=tl, n_len=n_len, k_transposed=k_transposed),
        grid=(n_seq, n_len),
        in_specs=[
            pl.BlockSpec((m, D_MODEL), row_map),
            _const_spec((D_MODEL, QKVU_WIDTH)),
            seq_spec,
            tab_spec, tab_spec, tab_spec,
            _const_spec((CONV_TAPS, CONV_CH)),
            _const_spec((1, CONV_CH)), _const_spec((1, CONV_CH)), _const_spec((1, CONV_CH)),
        ],
        out_specs=[out_rows(QK_WIDTH), k_spec, v_spec, out_rows(CONV_CH), seq_spec] + extra_specs,
        out_shape=[
            jax.ShapeDtypeStruct((rows, QK_WIDTH), BF16),
            jax.ShapeDtypeStruct(k_shape, F32),
            jax.ShapeDtypeStruct(v_shape, F32),
            jax.ShapeDtypeStruct((rows, CONV_CH), BF16),
            jax.ShapeDtypeStruct((n, CONV_TAPS - 1, CONV_CH), F32),
        ] + extra_shapes,
        scratch_shapes=[pltpu.VMEM((nb, CONV_PAD + tl, CONV_CH), F32)],
        compiler_params=pltpu.CompilerParams(dimension_semantics=("arbitrary", "arbitrary"),
                                             vmem_limit_bytes=VMEM_LIMIT),
        name="in_proj",
    )(x2, w_qkvu, ctx, cos, sa, sb, dww, dwb, lng, lnb)


def _subln(o1, l1, o2, l2, lam, g, out_scale):
    o = o1 / l1 - lam * (o2 / l2)
    ms = jnp.mean(o * o, axis=-1, keepdims=True)
    return o * lax.rsqrt(ms + LN_EPS) * g * out_scale


def _online_softmax_step(s, m_prev, l_prev):
    m_next = jnp.maximum(m_prev, jnp.max(s, axis=1, keepdims=True))
    p = jnp.exp2(s - m_next[:, :1])
    alpha = jnp.exp2(m_prev - m_next)
    l_next = alpha * l_prev + jnp.sum(p, axis=1, keepdims=True)
    return p, alpha, m_next, l_next


def _prompt_attn_kernel(qi_ref, ki_ref, lam_ref, q_ref, k_ref, v_ref, g_ref, o_ref,
                        qz_ref, m_ref, l_ref, acc_ref, s_ref, p_ref, a_ref, *, bq, bk, rc, out_scale):
    p_idx = pl.program_id(1)
    qi = qi_ref[p_idx]
    ki = ki_ref[p_idx]
    last_ki = (qi * bq + bq - 1) // bk
    n_tiles = bk // LANES

    @pl.when(ki == 0)
    def _():
        lane = lax.broadcasted_iota(jnp.int32, (bq, LANES), 1)
        for h in range(N_HEADS):
            qh = q_ref[:, h * LANES:(h + 1) * LANES]
            qz_ref[h, :bq, :] = jnp.where(lane < HEAD_DIM, qh, jnp.zeros_like(qh))
            qz_ref[h, bq:, :] = jnp.where(lane >= HEAD_DIM, qh, jnp.zeros_like(qh))
        m_ref[...] = jnp.full(m_ref.shape, NEG_BIG, F32)
        l_ref[...] = jnp.zeros(l_ref.shape, F32)
        acc_ref[...] = jnp.zeros(acc_ref.shape, F32)

    def body(masked):
        if masked:
            rel = lax.broadcasted_iota(jnp.int32, (rc, bk), 1) - lax.broadcasted_iota(jnp.int32, (rc, bk), 0)
            base = qi * bq - ki * bk
        for h in range(N_HEADS):
            s_ref[...] = _dot(qz_ref[h], k_ref[h * LANES:(h + 1) * LANES, :])
            for c in range(2 * bq // rc):
                rows = slice(c * rc, (c + 1) * rc)
                first_q = (c * rc) % bq
                nt = min(n_tiles, (first_q + rc - 1) // LANES + 1) if masked and bq == bk else n_tiles
                sc = s_ref[rows, :nt * LANES]
                if masked:
                    sc = jnp.where(rel[:, :nt * LANES] <= base + first_q, sc, NEG_BIG)
                tile_max = sc[:, :LANES]
                for j in range(1, nt):
                    tile_max = jnp.maximum(tile_max, sc[:, j * LANES:(j + 1) * LANES])
                m_prev = m_ref[h, rows, :]
                m_next = jnp.maximum(m_prev, jnp.max(tile_max, axis=1, keepdims=True))
                p = jnp.exp2(sc - jnp.concatenate([m_next] * nt, axis=1))
                alpha = jnp.exp2(m_prev - m_next)
                tile_sum = p[:, :LANES]
                for j in range(1, nt):
                    tile_sum = tile_sum + p[:, j * LANES:(j + 1) * LANES]
                l_ref[h, rows, :] = alpha * l_ref[h, rows, :] + tile_sum
                m_ref[h, rows, :] = m_next
                a_ref[rows, :] = alpha
                p_ref[rows, :nt * LANES] = p.astype(BF16)
                if nt < n_tiles:
                    p_ref[rows, nt * LANES:] = jnp.zeros((rc, (n_tiles - nt) * LANES), BF16)
            acc_ref[h] = a_ref[...] * acc_ref[h] + _dot(p_ref[...], v_ref[:, h * LANES:(h + 1) * LANES])

    on_diag = ki * bk + bk - 1 > qi * bq

    @pl.when(on_diag)
    def _():
        body(True)

    @pl.when(jnp.logical_not(on_diag))
    def _():
        body(False)

    @pl.when(ki == last_ki)
    def _():
        lam = lam_ref[0]
        for h in range(N_HEADS):
            l1 = jnp.sum(l_ref[h, :bq, :], axis=1, keepdims=True)
            l2 = jnp.sum(l_ref[h, bq:, :], axis=1, keepdims=True)
            o = _subln(acc_ref[h, :bq, :], l1, acc_ref[h, bq:, :], l2, lam, g_ref[...], out_scale)
            o_ref[:, h * LANES:(h + 1) * LANES] = o.astype(BF16)


def _prompt_attn(q, kt, v, lam, subln_g, out_scale, *, bq, bk, rc):
    b, s, _ = q.shape
    assert s % bq == 0 and s % bk == 0 and bq % rc == 0
    pairs = [(i, j) for i in range(s // bq) for j in range((i * bq + bq - 1) // bk + 1)]
    qi_tab = jnp.asarray(np.array([p[0] for p in pairs], np.int32))
    ki_tab = jnp.asarray(np.array([p[1] for p in pairs], np.int32))
    q_spec = pl.BlockSpec((None, bq, QK_WIDTH), lambda bi, p, qt, kt: (bi, qt[p], 0))
    k_spec = pl.BlockSpec((None, QK_WIDTH, bk), lambda bi, p, qt, kt: (bi, 0, kt[p]))
    v_spec = pl.BlockSpec((None, bk, ATTN_WIDTH), lambda bi, p, qt, kt: (bi, kt[p], 0))
    grid_spec = pltpu.PrefetchScalarGridSpec(
        num_scalar_prefetch=2,
        grid=(b, len(pairs)),
        in_specs=[
            pl.BlockSpec(memory_space=pltpu.SMEM),
            q_spec, k_spec, v_spec,
            pl.BlockSpec((1, V_DIM), lambda bi, p, qt, kt: (0, 0)),
        ],
        out_specs=q_spec,
        scratch_shapes=[
            pltpu.VMEM((N_HEADS, 2 * bq, LANES), BF16),
            pltpu.VMEM((N_HEADS, 2 * bq, LANES), F32),
            pltpu.VMEM((N_HEADS, 2 * bq, LANES), F32),
            pltpu.VMEM((N_HEADS, 2 * bq, V_DIM), F32),
            pltpu.VMEM((2 * bq, bk), F32),
            pltpu.VMEM((2 * bq, bk), BF16),
            pltpu.VMEM((2 * bq, LANES), F32),
        ],
    )
    return pl.pallas_call(
        functools.partial(_prompt_attn_kernel, bq=bq, bk=bk, rc=rc, out_scale=out_scale),
        grid_spec=grid_spec,
        out_shape=jax.ShapeDtypeStruct((b, s, ATTN_WIDTH), BF16),
        compiler_params=pltpu.CompilerParams(dimension_semantics=("arbitrary", "arbitrary"),
                                             vmem_limit_bytes=VMEM_LIMIT),
        name="prompt_attn",
    )(qi_tab, ki_tab, lam, q, kt, v, subln_g)


def _decode_attn_kernel(pt_ref, lam_ref, q_ref, kn_ref, vn_ref, g_ref, e_ref, hm_ref, kc_ref, vc_ref,
                        o_ref, qbd_ref, m_ref, l_ref, acc_ref, kbuf_ref, vbuf_ref, sem_ref,
                        *, n_seq, n_grp, grp, t_new, out_scale):
    i = pl.program_id(0)
    g = pl.program_id(1)
    step = i * n_grp + g
    slot = step % 2
    n_rows = 2 * N_HEADS * t_new
    rows_per_head = 2 * t_new

    def page_copies(seq, group, buf_slot):
        copies = []
        for j in range(grp):
            page = pt_ref[seq, group * grp + j]
            copies.append(pltpu.make_async_copy(kc_ref.at[page], kbuf_ref.at[buf_slot, j], sem_ref.at[0, buf_slot]))
            copies.append(pltpu.make_async_copy(vc_ref.at[page], vbuf_ref.at[buf_slot, j], sem_ref.at[1, buf_slot]))
        return copies

    @pl.when(step == 0)
    def _():
        for cp in page_copies(0, 0, 0):
            cp.start()

    @pl.when(step + 1 < n_seq * n_grp)
    def _():
        wrap = g + 1 == n_grp
        for cp in page_copies(jnp.where(wrap, i + 1, i), jnp.where(wrap, 0, g + 1), 1 - slot):
            cp.start()

    for cp in page_copies(i, g, slot):
        cp.wait()

    @pl.when(g == 0)
    def _():
        qf = q_ref[...].astype(F32)
        lane = lax.broadcasted_iota(jnp.int32, (t_new, QK_WIDTH), 1)
        parts = [jnp.where((lane >= r * HEAD_DIM) & (lane < (r + 1) * HEAD_DIM), qf, 0.0)
                 for r in range(2 * N_HEADS)]
        qbd_ref[...] = jnp.concatenate(parts, axis=0).astype(BF16)
        m_ref[...] = jnp.full(m_ref.shape, NEG_BIG, F32)
        l_ref[...] = jnp.zeros(l_ref.shape, F32)
        acc_ref[...] = jnp.zeros(acc_ref.shape, F32)

    def softmax_update(s):
        p, alpha, m_next, l_next = _online_softmax_step(s, m_ref[...], l_ref[...])
        m_ref[...] = m_next
        l_ref[...] = l_next
        return p.astype(BF16), alpha

    kt = jnp.concatenate([kbuf_ref[slot, j].astype(BF16) for j in range(grp)], axis=1)
    pb, alpha = softmax_update(_dot(qbd_ref[...], kt))
    p_rows = jnp.concatenate([pb[:, j * PAGE_SIZE:(j + 1) * PAGE_SIZE] for j in range(grp)], axis=0)
    p4 = _dot(p_rows, e_ref[...])
    hm = hm_ref[...]
    p4 = jnp.concatenate([(p4[j * n_rows:(j + 1) * n_rows, :] * hm).astype(BF16) for j in range(grp)], axis=1)
    vm = jnp.concatenate([vbuf_ref[slot, j].astype(BF16) for j in range(grp)], axis=0)
    acc_ref[...] = alpha * acc_ref[...] + _dot(p4, vm)

    @pl.when(g == n_grp - 1)
    def _():
        pad = jnp.zeros((LANES - t_new, QK_WIDTH), F32)
        kn = jnp.concatenate([kn_ref[...], pad], axis=0).astype(BF16)
        vn = jnp.concatenate([vn_ref[...], pad], axis=0).astype(BF16)
        row_t = lax.broadcasted_iota(jnp.int32, (n_rows, LANES), 0) % t_new
        col = lax.broadcasted_iota(jnp.int32, (n_rows, LANES), 1)
        s = jnp.where(col <= row_t, _dot_nt(qbd_ref[...], kn), NEG_BIG)
        pn, alpha_n = softmax_update(s)
        pv = _dot(pn, vn)
        pv = jnp.concatenate([pv[h * rows_per_head:(h + 1) * rows_per_head, h * LANES:(h + 1) * LANES]
                              for h in range(N_HEADS)], axis=0)
        acc = alpha_n * acc_ref[...] + pv
        lsum = l_ref[...]
        lam = lam_ref[0]
        for h in range(N_HEADS):
            r1 = slice(h * rows_per_head, h * rows_per_head + t_new)
            r2 = slice(h * rows_per_head + t_new, (h + 1) * rows_per_head)
            o = _subln(acc[r1], lsum[r1], acc[r2], lsum[r2], lam, g_ref[...], out_scale)
            o_ref[:, h * LANES:(h + 1) * LANES] = o.astype(BF16)


def _decode_attn(q, k_new, v_new, cache_kt, cache_vm, page_table, lam, subln_g, out_scale, *, grp):
    n, t_new, _ = q.shape
    n_pages = page_table.shape[1]
    assert n_pages % grp == 0
    n_grp = n_pages // grp
    n_rows = 2 * N_HEADS * t_new
    tok_spec = pl.BlockSpec((None, t_new, QK_WIDTH), lambda i, g, pt: (i, 0, 0))
    vrows = PAGE_SIZE * N_HEADS
    col = np.arange(vrows)
    expand = jnp.asarray(col[None, :] // N_HEADS == np.arange(PAGE_SIZE)[:, None], BF16)
    head_mask = jnp.asarray(col[None, :] % N_HEADS == (np.arange(n_rows) // (2 * t_new))[:, None], F32)

    grid_spec = pltpu.PrefetchScalarGridSpec(
        num_scalar_prefetch=1,
        grid=(n, n_grp),
        in_specs=[pl.BlockSpec(memory_space=pltpu.SMEM), tok_spec, tok_spec, tok_spec,
                  pl.BlockSpec((1, V_DIM), lambda i, g, pt: (0, 0)),
                  pl.BlockSpec((PAGE_SIZE, vrows), lambda i, g, pt: (0, 0)),
                  pl.BlockSpec((n_rows, vrows), lambda i, g, pt: (0, 0)),
                  pl.BlockSpec(memory_space=pl.ANY), pl.BlockSpec(memory_space=pl.ANY)],
        out_specs=tok_spec,
        scratch_shapes=[
            pltpu.VMEM((n_rows, QK_WIDTH), BF16),
            pltpu.VMEM((n_rows, LANES), F32),
            pltpu.VMEM((n_rows, LANES), F32),
            pltpu.VMEM((n_rows, V_DIM), F32),
            pltpu.VMEM((2, grp, QK_WIDTH, PAGE_SIZE), F32),
            pltpu.VMEM((2, grp, vrows, V_DIM), F32),
            pltpu.SemaphoreType.DMA((2, 2)),
        ],
    )
    return pl.pallas_call(
        functools.partial(_decode_attn_kernel, n_seq=n, n_grp=n_grp, grp=grp, t_new=t_new, out_scale=out_scale),
        grid_spec=grid_spec,
        out_shape=jax.ShapeDtypeStruct((n, t_new, ATTN_WIDTH), BF16),
        compiler_params=pltpu.CompilerParams(dimension_semantics=("arbitrary", "arbitrary"),
                                             vmem_limit_bytes=VMEM_LIMIT),
        name="decode_attn",
    )(page_table, lam, q, k_new, v_new, subln_g, expand, head_mask, cache_kt, cache_vm)


def _gelu_tanh(x):
    return 0.5 * x * (1.0 + jnp.tanh(math.sqrt(2.0 / math.pi) * (x + 0.044715 * (x * x * x))))


def _merge_ffn_kernel(x_ref, c_ref, o_ref, ctx_ref, wg_ref, wa_ref, wb_ref, wo_ref, g1_ref, b1_ref,
                      wup_ref, fw_ref, fb_ref, wdn_ref, g2_ref, b2_ref,
                      y_ref, nffn_ref, uext_ref, *, nb, tl, n_len, n_chunk, alpha):
    s = pl.program_id(1)
    lo = FFN_PAD - (FFN_TAPS - 1)
    m = nb * tl

    @pl.when(s == 0)
    def _():
        uext_ref[:, lo:FFN_PAD, :] = ctx_ref[...]

    x = x_ref[...]
    gates = _dot(x.astype(BF16), wg_ref[...])
    mixed = (jax.nn.sigmoid(gates[:, :D_MODEL]) * _dot(c_ref[...], wa_ref[...])
             + jax.nn.sigmoid(gates[:, D_MODEL:]) * _dot(o_ref[...], wb_ref[...]))
    h = _layer_norm(alpha * x + _dot(mixed.astype(BF16), wo_ref[...]), g1_ref[...], b1_ref[...])
    hb = h.astype(BF16)

    cw = D_FF // n_chunk
    f = jnp.zeros((m, D_MODEL), F32)
    for ci in range(n_chunk):
        cs = slice(ci * cw, (ci + 1) * cw)
        uf = _dot(hb, wup_ref[:, ci * cw:(ci + 1) * cw])
        vf = _dot(hb, wup_ref[:, D_FF + ci * cw:D_FF + (ci + 1) * cw])
        uf3 = uf.reshape(nb, tl, cw)
        uext_ref[:, FFN_PAD:FFN_PAD + tl, cs] = uf3
        cf = fb_ref[:, cs].reshape(1, 1, cw) + fw_ref[FFN_TAPS - 1:FFN_TAPS, cs].reshape(1, 1, cw) * uf3
        for j in range(FFN_TAPS - 1):
            cf = cf + fw_ref[j:j + 1, cs].reshape(1, 1, cw) * uext_ref[:, lo + j:lo + j + tl, cs]
        gl = _gelu_tanh(cf.reshape(m, cw)) * vf
        f = f + _dot(gl.astype(BF16), wdn_ref[ci * cw:(ci + 1) * cw, :])
    y_ref[...] = _layer_norm(alpha * h + f, g2_ref[...], b2_ref[...])

    nffn_ref[...] = uext_ref[:, lo + tl:FFN_PAD + tl, :]
    if n_len > 1:
        @pl.when(s + 1 < n_len)
        def _():
            uext_ref[:, lo:FFN_PAD, :] = uext_ref[:, lo + tl:FFN_PAD + tl, :]


def _merge_ffn(x2, c_act, o_n, ctx, w_gate, w_br_a, w_br_b, w_o, g1, b1, w_up, fw, fb, w_down, g2, b2,
               *, nb, tl, n_chunk, alpha):
    n = ctx.shape[0]
    length = x2.shape[0] // n
    assert n % nb == 0 and length % tl == 0 and D_FF % (n_chunk * LANES) == 0
    n_seq, n_len = n // nb, length // tl
    assert n_len == 1 or (nb == 1 and tl >= FFN_TAPS - 1)
    m = nb * tl
    row_map = lambda i, s: (i * n_len + s, 0)
    seq_spec = pl.BlockSpec((nb, FFN_TAPS - 1, D_FF), lambda i, s: (i, 0, 0))
    return pl.pallas_call(
        functools.partial(_merge_ffn_kernel, nb=nb, tl=tl, n_len=n_len, n_chunk=n_chunk, alpha=alpha),
        grid=(n_seq, n_len),
        in_specs=[
            pl.BlockSpec((m, D_MODEL), row_map),
            pl.BlockSpec((m, CONV_CH), row_map),
            pl.BlockSpec((m, ATTN_WIDTH), row_map),
            seq_spec,
            _const_spec((D_MODEL, GATE_WIDTH)),
            _const_spec((CONV_CH, D_MODEL)),
            _const_spec((ATTN_WIDTH, D_MODEL)),
            _const_spec((D_MODEL, D_MODEL)),
            _const_spec((1, D_MODEL)), _const_spec((1, D_MODEL)),
            _const_spec((D_MODEL, 2 * D_FF)),
            _const_spec((FFN_TAPS, D_FF)),
            _const_spec((1, D_FF)),
            _const_spec((D_FF, D_MODEL)),
            _const_spec((1, D_MODEL)), _const_spec((1, D_MODEL)),
        ],
        out_specs=[pl.BlockSpec((m, D_MODEL), row_map), seq_spec],
        out_shape=[jax.ShapeDtypeStruct((n * length, D_MODEL), F32),
                   jax.ShapeDtypeStruct((n, FFN_TAPS - 1, D_FF), F32)],
        scratch_shapes=[pltpu.VMEM((nb, FFN_PAD + tl, D_FF), F32)],
        compiler_params=pltpu.CompilerParams(dimension_semantics=("arbitrary", "arbitrary"),
                                             vmem_limit_bytes=VMEM_LIMIT),
        name="merge_ffn",
    )(x2, c_act, o_n, ctx, w_gate, w_br_a, w_br_b, w_o, g1, b1, w_up, fw, fb, w_down, g2, b2)


def _rope_tables(pos):
    half = ROT_DIM // 2
    inv = ROPE_THETA ** (-jnp.arange(half, dtype=F32) / half)
    dim = np.arange(LANES) % HEAD_DIM
    ang = pos.astype(F32)[:, None] * inv[dim % half][None, :]
    first = jnp.asarray(dim < half)[None, :]
    second = jnp.asarray((dim >= half) & (dim < ROT_DIM))[None, :]
    cos, sin = jnp.cos(ang), jnp.sin(ang)
    return (jnp.where(first | second, cos, 1.0), jnp.where(first, -sin, 0.0), jnp.where(second, sin, 0.0))


def _tile_len(length, cap):
    t = min(length, cap)
    while length % t:
        t //= 2
    return t


def kernel(x_prompt, x_sample, cache_k, cache_v, state_conv, state_ffn, page_table, w_in, lambda_q1, lambda_k1,
           lambda_q2, lambda_k2, conv_dw_w, conv_dw_b, conv_ln_g, conv_ln_b, w_br_a, subln_g, w_br_b, w_o,
           ln1_g, ln1_b, w_up, ffn_dw_w, ffn_dw_b, w_down, ln2_g, ln2_b):
    b, s, _ = x_prompt.shape
    db, t, _ = x_sample.shape
    depth = w_in.shape[0]
    n_pages = page_table.shape[1]
    past = n_pages * PAGE_SIZE
    alpha = (2 * depth) ** 0.25
    n_phys = cache_k.shape[1]

    tabs_p = _rope_tables(jnp.arange(s, dtype=jnp.int32))
    tabs_s = tuple(jnp.tile(tb, (db, 1)) for tb in _rope_tables(past + jnp.arange(t, dtype=jnp.int32)))
    conv_zero = jnp.zeros((b, CONV_TAPS - 1, CONV_CH), F32)
    ffn_zero = jnp.zeros((b, FFN_TAPS - 1, D_FF), F32)

    tl_p = _tile_len(s, 512)
    bq = _tile_len(s, 512)
    grp = math.gcd(n_pages, 32)
    nb_s = db
    while nb_s * t > 256 and nb_s % 2 == 0:
        nb_s //= 2

    yp = x_prompt.reshape(b * s, D_MODEL)
    ys = x_sample.reshape(db * t, D_MODEL)
    outs = [[] for _ in range(8)]
    row = lambda a: a.reshape(1, -1)
    for l in range(depth):
        lam_init = 0.8 - 0.6 * math.exp(-0.3 * l)
        lam = (jnp.exp(jnp.sum(lambda_q1[l] * lambda_k1[l])) - jnp.exp(jnp.sum(lambda_q2[l] * lambda_k2[l]))
               + lam_init).reshape(1).astype(F32)
        out_scale = 1.0 - lam_init
        w_qkvu = w_in[l][:, :QKVU_WIDTH].astype(BF16)
        w_gate = w_in[l][:, QKVU_WIDTH:].astype(BF16)
        conv_w = (conv_dw_w[l], row(conv_dw_b[l]), row(conv_ln_g[l]), row(conv_ln_b[l]))
        ffn_w = (w_gate, w_br_a[l].astype(BF16), w_br_b[l].astype(BF16), w_o[l].astype(BF16), row(ln1_g[l]),
                 row(ln1_b[l]), w_up[l].astype(BF16), ffn_dw_w[l], row(ffn_dw_b[l]), w_down[l].astype(BF16),
                 row(ln2_g[l]), row(ln2_b[l]))
        g_sub = row(subln_g[l])

        q, kt, vf, c_act, cp, ktb, vb = _in_proj(yp, conv_zero, w_qkvu, *tabs_p, *conv_w, nb=1, tl=tl_p,
                                                 k_transposed=True)
        o_n = _prompt_attn(q.reshape(b, s, QK_WIDTH), ktb, vb.reshape(b, s, ATTN_WIDTH),
                           lam, g_sub, out_scale, bq=bq, bk=bq, rc=min(bq, 32))
        yp, fp = _merge_ffn(yp, c_act, o_n.reshape(b * s, ATTN_WIDTH), ffn_zero, *ffn_w,
                            nb=1, tl=tl_p, n_chunk=1, alpha=alpha)
        outs[0].append(jnp.transpose(kt.reshape(b, N_HEADS, 2, HEAD_DIM, s), (0, 4, 1, 2, 3)))
        outs[1].append(vf.reshape(b, s, N_HEADS, V_DIM))
        outs[2].append(cp)
        outs[3].append(fp)

        q, kf, vf, c_act, cs = _in_proj(ys, state_conv[l], w_qkvu, *tabs_s, *conv_w, nb=nb_s, tl=t,
                                        k_transposed=False)
        cache_kt = jnp.transpose(cache_k[l], (0, 2, 3, 4, 1)).reshape(n_phys, QK_WIDTH, PAGE_SIZE)
        cache_vm = cache_v[l].reshape(n_phys, PAGE_SIZE * N_HEADS, V_DIM)
        o_n = _decode_attn(q.reshape(db, t, QK_WIDTH), kf.reshape(db, t, QK_WIDTH), vf.reshape(db, t, ATTN_WIDTH),
                           cache_kt, cache_vm, page_table, lam, g_sub, out_scale, grp=grp)
        ys, fs = _merge_ffn(ys, c_act, o_n.reshape(db * t, ATTN_WIDTH), state_ffn[l], *ffn_w,
                            nb=nb_s, tl=t, n_chunk=1, alpha=alpha)
        outs[4].append(kf.reshape(db, t, N_HEADS, 2, HEAD_DIM))
        outs[5].append(vf.reshape(db, t, N_HEADS, V_DIM))
        outs[6].append(cs)
        outs[7].append(fs)

    stacked = [jnp.stack(o) for o in outs]
    return (yp.reshape(b, s, D_MODEL), ys.reshape(db, t, D_MODEL), *stacked)
```
